```python
import jax, jax.numpy as jnp
from jax import lax
import numpy as np

D_MODEL = 1024
BATCH = 2
SEQ = 8192
DEPTH = 4

GLA_HEADS = 4
GLA_DK = 128
GLA_DV = 128
GLA_LOWRANK = 16
GLA_GATE_NORM = 16.0
GLA_CHUNK = 64
MOBA_HEADS = 8
MOBA_HD = 64
MOBA_BLOCK = 256
MOBA_TOPK = 3
MOBA_QCHUNK = 64
ROPE_THETA = 10000.0
MLSTM_HEADS = 4
MLSTM_DQK = 64
MLSTM_DV = 128
MLSTM_CHUNK = 64
MLSTM_CONV = 4
D_FF = 4 * D_MODEL
N_BRANCH = 3
EPS = 1e-6

GLA_K = GLA_HEADS * GLA_DK
GLA_V = GLA_HEADS * GLA_DV
MOBA_W = MOBA_HEADS * MOBA_HD
MLSTM_QK = MLSTM_HEADS * MLSTM_DQK
MLSTM_V = MLSTM_HEADS * MLSTM_DV
IN_SPLITS = (GLA_K, GLA_K, GLA_V, GLA_V, GLA_LOWRANK,
             MOBA_W, MOBA_W, MOBA_W,
             MLSTM_QK, MLSTM_QK, MLSTM_V, MLSTM_V, MLSTM_HEADS, MLSTM_HEADS,
             N_BRANCH * D_MODEL)
D_IN = sum(IN_SPLITS)

kernel_name = "hybrid_gla_moba_mlstm_block"


def rms_norm(x, g):
    xf = x.astype(jnp.float32)
    y = xf * lax.rsqrt(jnp.mean(xf * xf, axis=-1, keepdims=True) + EPS)
    return (y * g.astype(jnp.float32)).astype(x.dtype)


def split_heads(x, h):
    b, t, _ = x.shape
    return x.reshape(b, t, h, -1).transpose(0, 2, 1, 3)


def merge_heads(x):
    b, h, t, d = x.shape
    return x.transpose(0, 2, 1, 3).reshape(b, t, h * d)


def rope_tables(seq, dim):
    inv = 1.0 / (ROPE_THETA ** (jnp.arange(0, dim, 2, dtype=jnp.float32) / dim))
    ang = jnp.arange(seq, dtype=jnp.float32)[:, None] * inv[None, :]
    return jnp.cos(ang), jnp.sin(ang)


def apply_rope(x, cos, sin):
    x1, x2 = jnp.split(x, 2, axis=-1)
    return jnp.concatenate([x1 * cos - x2 * sin, x2 * cos + x1 * sin], axis=-1)


def causal_dwconv(x, w):
    wd, c = w.shape
    return lax.conv_general_dilated(x, w[:, None, :], window_strides=(1,), padding=((wd - 1, 0),),
                                    dimension_numbers=('NWC', 'WIO', 'NWC'), feature_group_count=c)


def gla_mixer(q, k, v, log_a):
    b, h, t, dk = q.shape
    dv = v.shape[-1]
    L = GLA_CHUNK
    n = t // L
    qc = (q * dk ** -0.5).reshape(b, h, n, L, dk)
    kc = k.reshape(b, h, n, L, dk)
    vc = v.reshape(b, h, n, L, dv)
    bcum = jnp.cumsum(log_a.reshape(b, h, n, L, dk), axis=3)
    b_last = bcum[:, :, :, -1:, :]
    q_dec = qc * jnp.exp(bcum)
    k_inv = kc * jnp.exp(-bcum)
    k_end = kc * jnp.exp(b_last - bcum)
    causal = jnp.tril(jnp.ones((L, L), dtype=bool))
    attn = jnp.where(causal, jnp.einsum('bhnid,bhnjd->bhnij', q_dec, k_inv), 0.0)
    o_intra = jnp.einsum('bhnij,bhnjv->bhniv', attn, vc)
    chunk_kv = jnp.einsum('bhnld,bhnlv->bhndv', k_end, vc)
    decay = jnp.exp(b_last[:, :, :, 0, :])

    def step(S, inp):
        dec, kv = inp
        return dec[..., None] * S + kv, S

    S0 = jnp.zeros((b, h, dk, dv), q.dtype)
    _, S_prev = lax.scan(step, S0, (jnp.moveaxis(decay, 2, 0), jnp.moveaxis(chunk_kv, 2, 0)))
    S_prev = jnp.moveaxis(S_prev, 0, 2)
    o_inter = jnp.einsum('bhnld,bhndv->bhnlv', q_dec, S_prev)
    return (o_intra + o_inter).reshape(b, h, t, dv)


def moba_mixer(q, k, v):
    b, h, t, d = q.shape
    t_pad = -(-t // MOBA_BLOCK) * MOBA_BLOCK
    pad = ((0, 0), (0, 0), (0, t_pad - t), (0, 0))
    q, k, v = jnp.pad(q, pad), jnp.pad(k, pad), jnp.pad(v, pad)
    nb = t_pad // MOBA_BLOCK
    k_sel = min(MOBA_TOPK, nb)
    scale = d ** -0.5
    k_blocks = k.reshape(b, h, nb, MOBA_BLOCK, d)
    v_blocks = v.reshape(b, h, nb, MOBA_BLOCK, d)
    k_mean = jnp.mean(k_blocks, axis=3)
    pos = jnp.arange(t_pad)
    past = jnp.arange(nb)[None, :] < (pos // MOBA_BLOCK)[:, None]
    gate = jnp.einsum('bhtd,bhnd->bhtn', q, k_mean)
    gate = jnp.where(past, gate, -jnp.inf)
    top_val, top_idx = lax.top_k(gate, k_sel)
    valid = jnp.isfinite(top_val)
    bi = jnp.arange(b)[:, None, None, None]
    hi = jnp.arange(h)[None, :, None, None]
    qn = MOBA_QCHUNK

    def chunk_fn(c):
        start = c * qn
        qc = lax.dynamic_slice_in_dim(q, start, qn, axis=2)
        idx = lax.dynamic_slice_in_dim(top_idx, start, qn, axis=2)
        ok = lax.dynamic_slice_in_dim(valid, start, qn, axis=2)
        kg = k_blocks[bi, hi, idx]
        vg = v_blocks[bi, hi, idx]
        s_sel = jnp.einsum('bhqd,bhqskd->bhqsk', qc, kg) * scale
        s_sel = jnp.where(ok[..., None], s_sel, -jnp.inf).reshape(b, h, qn, k_sel * MOBA_BLOCK)
        own_start = (start // MOBA_BLOCK) * MOBA_BLOCK
        ko = lax.dynamic_slice_in_dim(k, own_start, MOBA_BLOCK, axis=2)
        vo = lax.dynamic_slice_in_dim(v, own_start, MOBA_BLOCK, axis=2)
        s_own = jnp.einsum('bhqd,bhkd->bhqk', qc, ko) * scale
        qpos = start + jnp.arange(qn)
        kpos = own_start + jnp.arange(MOBA_BLOCK)
        s_own = jnp.where(kpos[None, :] <= qpos[:, None], s_own, -jnp.inf)
        p = jax.nn.softmax(jnp.concatenate([s_sel, s_own], axis=-1).astype(jnp.float32), axis=-1)
        p_sel = p[..., :k_sel * MOBA_BLOCK].reshape(b, h, qn, k_sel, MOBA_BLOCK)
        p_own = p[..., k_sel * MOBA_BLOCK:]
        return (jnp.einsum('bhqsk,bhqskd->bhqd', p_sel, vg)
                + jnp.einsum('bhqk,bhkd->bhqd', p_own, vo))

    outs = lax.map(chunk_fn, jnp.arange(t_pad // qn))
    out = outs.transpose(1, 2, 0, 3, 4).reshape(b, h, t_pad, d)
    return out[:, :, :t]


def mlstm_mixer(q, k, v, log_i, log_f):
    b, h, t, dk = q.shape
    dv = v.shape[-1]
    L = MLSTM_CHUNK
    n = t // L
    qc = q.reshape(b, h, n, L, dk)
    kc = (k * dk ** -0.5).reshape(b, h, n, L, dk)
    vc = v.reshape(b, h, n, L, dv)
    li = log_i.reshape(b, h, n, L)
    bcum = jnp.cumsum(log_f.reshape(b, h, n, L), axis=-1)
    b_last = bcum[..., -1]
    causal = jnp.tril(jnp.ones((L, L), dtype=bool))
    logD = jnp.where(causal, bcum[..., :, None] - bcum[..., None, :] + li[..., None, :], -jnp.inf)
    log_end = b_last[..., None] - bcum + li
    m_local = jnp.max(log_end, axis=-1)
    w_loc = jnp.exp(log_end - m_local[..., None])
    chunk_kv = jnp.einsum('bhnl,bhnld,bhnlv->bhndv', w_loc, kc, vc)
    chunk_k = jnp.einsum('bhnl,bhnld->bhnd', w_loc, kc)

    def step(carry, inp):
        S, nv, m = carry
        bl, ml, ckv, ck = inp
        m_new = jnp.maximum(bl + m, ml)
        a = jnp.exp(bl + m - m_new)
        c = jnp.exp(ml - m_new)
        S_new = a[..., None, None] * S + c[..., None, None] * ckv
        n_new = a[..., None] * nv + c[..., None] * ck
        return (S_new, n_new, m_new), (S, nv, m)

    init = (jnp.zeros((b, h, dk, dv), q.dtype), jnp.zeros((b, h, dk), q.dtype), jnp.zeros((b, h), q.dtype))
    xs = (jnp.moveaxis(b_last, 2, 0), jnp.moveaxis(m_local, 2, 0),
          jnp.moveaxis(chunk_kv, 2, 0), jnp.moveaxis(chunk_k, 2, 0))
    _, (S_prev, n_prev, m_prev) = lax.scan(step, init, xs)
    S_prev = jnp.moveaxis(S_prev, 0, 2)
    n_prev = jnp.moveaxis(n_prev, 0, 2)
    m_prev = jnp.moveaxis(m_prev, 0, 2)
    m_inter = bcum + m_prev[..., None]
    m_row = jnp.maximum(m_inter, jnp.max(logD, axis=-1))
    s = jnp.einsum('bhnid,bhnjd->bhnij', qc, kc) * jnp.exp(logD - m_row[..., None])
    inter_w = jnp.exp(m_inter - m_row)
    num = (jnp.einsum('bhnij,bhnjv->bhniv', s, vc)
           + inter_w[..., None] * jnp.einsum('bhnid,bhndv->bhniv', qc, S_prev))
    den = jnp.sum(s, axis=-1) + inter_w * jnp.einsum('bhnid,bhnd->bhni', qc, n_prev)
    hid = num / jnp.maximum(jnp.abs(den), jnp.exp(-m_row))[..., None]
    return hid.reshape(b, h, t, dv)


def hybrid_layer(x, cos, sin, norm1_g, w_in, gla_a_up, gla_a_b, gla_norm_g, moba_qn_g, moba_kn_g,
                 mlstm_conv_w, mlstm_i_b, mlstm_f_b, mlstm_norm_g, gate_b, w_br_gla, w_br_moba,
                 w_br_mlstm, w_out, norm2_g, w_ff1, w_ff2):
    dt = x.dtype
    f32 = jnp.float32
    bsz, t, _ = x.shape
    hn = rms_norm(x, norm1_g)
    proj = hn @ w_in
    offsets = np.cumsum(IN_SPLITS)[:-1].tolist()
    (g_q, g_k, g_v, g_g, g_a, m_q, m_k, m_v,
     l_q, l_k, l_v, l_o, l_i, l_f, br_gate) = jnp.split(proj, offsets, axis=-1)

    log_a = jax.nn.log_sigmoid((g_a @ gla_a_up + gla_a_b).astype(f32)) / GLA_GATE_NORM
    o_gla = gla_mixer(split_heads(g_q.astype(f32), GLA_HEADS), split_heads(g_k.astype(f32), GLA_HEADS),
                      split_heads(g_v.astype(f32), GLA_HEADS), split_heads(log_a, GLA_HEADS))
    o_gla = rms_norm(o_gla, gla_norm_g)
    y_gla = (merge_heads(o_gla) * jax.nn.silu(g_g.astype(f32))).astype(dt)

    mq = apply_rope(rms_norm(split_heads(m_q.astype(f32), MOBA_HEADS), moba_qn_g), cos, sin)
    mk = apply_rope(rms_norm(split_heads(m_k.astype(f32), MOBA_HEADS), moba_kn_g), cos, sin)
    mv = split_heads(m_v.astype(f32), MOBA_HEADS)
    y_moba = merge_heads(moba_mixer(mq, mk, mv)).astype(dt)

    qk = causal_dwconv(jnp.concatenate([l_q, l_k], axis=-1).astype(f32), mlstm_conv_w.astype(f32))
    lq, lk = jnp.split(qk, 2, axis=-1)
    log_i = (l_i + mlstm_i_b).astype(f32).transpose(0, 2, 1)
    log_f = jax.nn.log_sigmoid((l_f + mlstm_f_b).astype(f32)).transpose(0, 2, 1)
    h_ml = mlstm_mixer(split_heads(lq, MLSTM_HEADS), split_heads(lk, MLSTM_HEADS),
                       split_heads(l_v.astype(f32), MLSTM_HEADS), log_i, log_f)
    h_ml = rms_norm(h_ml, mlstm_norm_g)
    y_mlstm = (merge_heads(h_ml) * jax.nn.sigmoid(l_o.astype(f32))).astype(dt)

    gates = jax.nn.sigmoid(br_gate + gate_b).reshape(bsz, t, N_BRANCH, D_MODEL)
    mixed = (gates[:, :, 0] * (y_gla @ w_br_gla)
             + gates[:, :, 1] * (y_moba @ w_br_moba)
             + gates[:, :, 2] * (y_mlstm @ w_br_mlstm))
    x = x + mixed @ w_out

    h2 = rms_norm(x, norm2_g)
    x = x + jnp.square(jax.nn.relu(h2 @ w_ff1)) @ w_ff2
    return x


def setup_inputs(seed: int = 0) -> dict:
    key = jax.random.key(seed)
    ks = jax.random.split(key, 24)

    def nrm(k, shape, fan_in):
        return jax.random.normal(k, shape, jnp.float32) * (fan_in ** -0.5)

    def gain(k, shape):
        return 1.0 + 0.02 * jax.random.normal(k, shape, jnp.float32)

    f_bias = (jnp.linspace(3.0, 6.0, MLSTM_HEADS, dtype=jnp.float32)[None, :]
              + 0.1 * jax.random.normal(ks[10], (DEPTH, MLSTM_HEADS), jnp.float32))
    return {
        "x": jax.random.normal(ks[0], (BATCH, SEQ, D_MODEL), jnp.float32),
        "norm1_g": gain(ks[1], (DEPTH, D_MODEL)),
        "w_in": nrm(ks[2], (DEPTH, D_MODEL, D_IN), D_MODEL),
        "gla_a_up": nrm(ks[3], (DEPTH, GLA_LOWRANK, GLA_K), GLA_LOWRANK),
        "gla_a_b": 0.02 * jax.random.normal(ks[4], (DEPTH, GLA_K), jnp.float32),
        "gla_norm_g": gain(ks[5], (DEPTH, GLA_DV)),
        "moba_qn_g": gain(ks[6], (DEPTH, MOBA_HD)),
        "moba_kn_g": gain(ks[7], (DEPTH, MOBA_HD)),
        "mlstm_conv_w": nrm(ks[8], (DEPTH, MLSTM_CONV, 2 * MLSTM_QK), MLSTM_CONV),
        "mlstm_i_b": 0.1 * jax.random.normal(ks[9], (DEPTH, MLSTM_HEADS), jnp.float32),
        "mlstm_f_b": f_bias,
        "mlstm_norm_g": gain(ks[11], (DEPTH, MLSTM_DV)),
        "gate_b": 0.02 * jax.random.normal(ks[12], (DEPTH, N_BRANCH * D_MODEL), jnp.float32),
        "w_br_gla": nrm(ks[13], (DEPTH, GLA_V, D_MODEL), GLA_V),
        "w_br_moba": nrm(ks[14], (DEPTH, MOBA_W, D_MODEL), MOBA_W),
        "w_br_mlstm": nrm(ks[15], (DEPTH, MLSTM_V, D_MODEL), MLSTM_V),
        "w_out": nrm(ks[16], (DEPTH, D_MODEL, D_MODEL), D_MODEL),
        "norm2_g": gain(ks[17], (DEPTH, D_MODEL)),
        "w_ff1": nrm(ks[18], (DEPTH, D_MODEL, D_FF), D_MODEL),
        "w_ff2": nrm(ks[19], (DEPTH, D_FF, D_MODEL), D_FF),
    }


def reference(x, norm1_g, w_in, gla_a_up, gla_a_b, gla_norm_g, moba_qn_g, moba_kn_g, mlstm_conv_w,
              mlstm_i_b, mlstm_f_b, mlstm_norm_g, gate_b, w_br_gla, w_br_moba, w_br_mlstm, w_out,
              norm2_g, w_ff1, w_ff2):
    cos, sin = rope_tables(x.shape[1], MOBA_HD)
    for l in range(DEPTH):
        x = hybrid_layer(x, cos, sin, norm1_g[l], w_in[l], gla_a_up[l], gla_a_b[l], gla_norm_g[l],
                         moba_qn_g[l], moba_kn_g[l], mlstm_conv_w[l], mlstm_i_b[l], mlstm_f_b[l],
                         mlstm_norm_g[l], gate_b[l], w_br_gla[l], w_br_moba[l], w_br_mlstm[l], w_out[l],
                         norm2_g[l], w_ff1[l], w_ff2[l])
    return x
```

```python
import functools

import jax
import jax.numpy as jnp
from jax import lax
from jax.experimental import pallas as pl
from jax.experimental.pallas import tpu as pltpu

F32 = jnp.float32
BF16 = jnp.bfloat16
HI = lax.Precision.HIGHEST

D_MODEL = 1024
GLA_HEADS, GLA_DK, GLA_DV, GLA_LOWRANK = 4, 128, 128, 16
GLA_GATE_NORM = 16.0
GLA_CHUNK = 64
MOBA_HEADS, MOBA_HD, MOBA_BLOCK, MOBA_TOPK = 8, 64, 256, 3
ROPE_THETA = 10000.0
MLSTM_HEADS, MLSTM_DQK, MLSTM_DV = 4, 64, 128
MLSTM_CHUNK = 64
MLSTM_CONV = 4
D_FF = 4 * D_MODEL
N_BRANCH = 3
EPS = 1e-6

GLA_W = GLA_HEADS * GLA_DK
MOBA_W = MOBA_HEADS * MOBA_HD
MLSTM_QK = MLSTM_HEADS * MLSTM_DQK
MLSTM_V = MLSTM_HEADS * MLSTM_DV

COL_GLA_Q, COL_GLA_K, COL_GLA_V, COL_GLA_G = 0, 1, 2, 3
COL_MOBA_Q, COL_MOBA_K, COL_MOBA_V = 4, 5, 6
COL_ML_QK, COL_ML_V, COL_ML_O = 7, 8, 9
COL_GATE0 = 5
N_MAIN = 8192
SM_GA, SM_LI, SM_LF = 0, 16, 20
SMALL_W = 128

VMEM_LIMIT = 48 * 1024 * 1024


def _dot(a, b, precision=None):
    return jnp.dot(a, b, preferred_element_type=F32, precision=precision)


def _dot_nt(a, b):
    return lax.dot_general(a, b, (((1,), (1,)), ((), ())), preferred_element_type=F32)


def _dot_tn(a, b):
    return lax.dot_general(a, b, (((0,), (0,)), ((), ())), preferred_element_type=F32)


def _sigmoid(x):
    return 1.0 / (1.0 + jnp.exp(-x))


def _log_sigmoid(x):
    return jnp.minimum(x, 0.0) - jnp.log(1.0 + jnp.exp(-jnp.abs(x)))


def _params(*sem):
    return pltpu.CompilerParams(dimension_semantics=sem, vmem_limit_bytes=VMEM_LIMIT)


def _inproj_kernel(x_ref, g_ref, w_ref, ws_ref, out_ref, small_ref, hn_ref):
    @pl.when(pl.program_id(1) == 0)
    def _():
        x = x_ref[...]
        ms = jnp.mean(x * x, axis=-1, keepdims=True)
        hn = (x * lax.rsqrt(ms + EPS) * g_ref[...]).astype(BF16)
        hn_ref[...] = hn
        small_ref[...] = _dot(hn, ws_ref[...])

    out_ref[...] = _dot(hn_ref[...], w_ref[...])


def _inproj(x, g, w_main, w_small, tm, tn):
    n = x.shape[0]
    return pl.pallas_call(
        _inproj_kernel,
        grid=(n // tm, N_MAIN // tn),
        in_specs=[
            pl.BlockSpec((tm, D_MODEL), lambda i, j: (i, 0)),
            pl.BlockSpec((1, D_MODEL), lambda i, j: (0, 0)),
            pl.BlockSpec((D_MODEL, tn), lambda i, j: (0, j)),
            pl.BlockSpec((D_MODEL, SMALL_W), lambda i, j: (0, 0)),
        ],
        out_specs=[
            pl.BlockSpec((tm, tn), lambda i, j: (i, j)),
            pl.BlockSpec((tm, SMALL_W), lambda i, j: (i, 0)),
        ],
        out_shape=[
            jax.ShapeDtypeStruct((n, N_MAIN), F32),
            jax.ShapeDtypeStruct((n, SMALL_W), F32),
        ],
        scratch_shapes=[pltpu.VMEM((tm, D_MODEL), BF16)],
        compiler_params=_params("parallel", "arbitrary"),
        name="inproj",
    )(x, g, w_main, w_small)


def _gla_kernel(q_ref, k_ref, v_ref, g_ref, small_ref, aup_ref, ab_ref, ng_ref, tril_ref,
                o_ref, st_ref, la_ref):
    tb = q_ref.shape[0]
    L = GLA_CHUNK

    @pl.when(pl.program_id(1) == 0)
    def _():
        st_ref[...] = jnp.zeros_like(st_ref)

    z = _dot(small_ref[...], aup_ref[...], precision=HI) + ab_ref[...]
    la_ref[...] = _log_sigmoid(z) * (1.0 / GLA_GATE_NORM)

    tril = tril_ref[...]
    causal = (lax.broadcasted_iota(jnp.int32, (L, L), 0)
              >= lax.broadcasted_iota(jnp.int32, (L, L), 1))
    scale = GLA_DK ** -0.5
    ng = ng_ref[...]

    def body(c, carry):
        rows = pl.ds(pl.multiple_of(c * L, L), L)
        for h in range(GLA_HEADS):
            cq = slice(h * GLA_DK, (h + 1) * GLA_DK)
            cv = slice(h * GLA_DV, (h + 1) * GLA_DV)
            bcum = _dot(tril, la_ref[rows, cq], precision=HI)
            b_last = bcum[L - 1:L, :]
            q = q_ref[rows, cq]
            k = k_ref[rows, cq]
            v = v_ref[rows, cv].astype(BF16)
            q_dec = (q * scale * jnp.exp(bcum)).astype(BF16)
            k_inv = (k * jnp.exp(-bcum)).astype(BF16)
            k_end = (k * jnp.exp(b_last - bcum)).astype(BF16)
            attn = jnp.where(causal, _dot_nt(q_dec, k_inv), 0.0).astype(BF16)
            st = st_ref[h]
            o = _dot(attn, v) + _dot_nt(q_dec, st.astype(BF16))
            st_ref[h] = st * jnp.exp(b_last) + _dot_tn(v, k_end)
            ms = jnp.mean(o * o, axis=-1, keepdims=True)
            y = o * lax.rsqrt(ms + EPS) * ng
            gg = g_ref[rows, cv]
            o_ref[rows, cv] = (y * (gg * _sigmoid(gg))).astype(o_ref.dtype)
        return carry

    lax.fori_loop(0, tb // L, body, 0)


def _gla(proj, small, a_up, a_b, norm_g, tril, bsz, t, tb):
    n = proj.shape[0]
    nt = t // tb

    def col(cidx):
        return pl.BlockSpec((tb, GLA_W), lambda b, s: (b * nt + s, cidx))

    const = lambda shape: pl.BlockSpec(shape, lambda b, s: (0,) * len(shape))
    return pl.pallas_call(
        _gla_kernel,
        grid=(bsz, nt),
        in_specs=[
            col(COL_GLA_Q), col(COL_GLA_K), col(COL_GLA_V), col(COL_GLA_G),
            pl.BlockSpec((tb, SMALL_W), lambda b, s: (b * nt + s, 0)),
            const((SMALL_W, GLA_W)), const((1, GLA_W)), const((1, GLA_DV)),
            const((GLA_CHUNK, GLA_CHUNK)),
        ],
        out_specs=pl.BlockSpec((tb, GLA_W), lambda b, s: (b * nt + s, 0)),
        out_shape=jax.ShapeDtypeStruct((n, GLA_W), BF16),
        scratch_shapes=[
            pltpu.VMEM((GLA_HEADS, GLA_DV, GLA_DK), F32),
            pltpu.VMEM((tb, GLA_W), F32),
        ],
        compiler_params=_params("parallel", "arbitrary"),
        name="gla",
    )(proj, proj, proj, proj, small, a_up, a_b, norm_g, tril)


def _mlstm_kernel(qk_ref, v_ref, lo_ref, small_ref, cw_ref, gb_ref, ng_ref, tril_ref,
                  out_ref, ext_ref, qkc_ref, gc_ref, st_ref, m_ref):
    tb = qk_ref.shape[0]
    L = MLSTM_CHUNK
    pad = 8

    @pl.when(pl.program_id(1) == 0)
    def _():
        st_ref[...] = jnp.zeros_like(st_ref)
        m_ref[...] = jnp.zeros_like(m_ref)
        ext_ref[0:pad, :] = jnp.zeros((pad, ext_ref.shape[1]), F32)

    x = qk_ref[...]
    ext_ref[pad:pad + tb, :] = x
    acc = None
    for w in range(MLSTM_CONV):
        off = pad - (MLSTM_CONV - 1) + w
        term = ext_ref[off:off + tb, :] * cw_ref[w:w + 1, :]
        acc = term if acc is None else acc + term
    qkc_ref[...] = acc
    ext_ref[0:pad, :] = x[tb - pad:tb, :]

    gsm = small_ref[...] + gb_ref[...]
    lane = lax.broadcasted_iota(jnp.int32, gsm.shape, 1)
    is_f = (lane >= SM_LF) & (lane < SM_LF + MLSTM_HEADS)
    gc_ref[...] = jnp.where(is_f, _log_sigmoid(gsm), gsm)

    tril = tril_ref[...]
    causal = (lax.broadcasted_iota(jnp.int32, (L, L), 0)
              >= lax.broadcasted_iota(jnp.int32, (L, L), 1))
    lane_c = lax.broadcasted_iota(jnp.int32, (L, SMALL_W), 1)
    is_f_c = (lane_c >= SM_LF) & (lane_c < SM_LF + MLSTM_HEADS)
    ones_aug = jnp.ones((L, MLSTM_DV), BF16)
    ng = ng_ref[...]
    kscale = MLSTM_DQK ** -0.5

    def body(c, carry):
        rows = pl.ds(pl.multiple_of(c * L, L), L)
        gates = gc_ref[rows, :]
        cum = _dot(tril, gates, precision=HI)
        pc = jnp.where(is_f_c, cum, gates)
        pr = pc.T
        for h in range(MLSTM_HEADS):
            bc = pc[:, SM_LF + h:SM_LF + h + 1]
            lic = pc[:, SM_LI + h:SM_LI + h + 1]
            br = pr[SM_LF + h:SM_LF + h + 1, :]
            lir = pr[SM_LI + h:SM_LI + h + 1, :]
            b_last = br[:, L - 1:L]
            log_d = jnp.where(causal, bc - br + lir, -jnp.inf)
            max_d = jnp.max(log_d, axis=-1, keepdims=True)
            m_loc = jnp.max(b_last - br + lir, axis=-1, keepdims=True)
            w_c = jnp.exp(b_last - bc + lic - m_loc)
            m_prev = m_ref[h:h + 1, 0:1]
            m_inter = bc + m_prev
            m_row = jnp.maximum(m_inter, max_d)

            q = qkc_ref[rows, h * MLSTM_DQK:(h + 1) * MLSTM_DQK].astype(BF16)
            kf = qkc_ref[rows, MLSTM_QK + h * MLSTM_DQK:MLSTM_QK + (h + 1) * MLSTM_DQK] * kscale
            cv = slice(h * MLSTM_DV, (h + 1) * MLSTM_DV)
            v_aug = jnp.concatenate([v_ref[rows, cv].astype(BF16), ones_aug], axis=1)

            s = _dot_nt(q, kf.astype(BF16)) * jnp.exp(log_d - m_row)
            inter_w = jnp.exp(m_inter - m_row)
            st = st_ref[h]
            tot = _dot(s.astype(BF16), v_aug) + inter_w * _dot(q, st.astype(BF16))
            num = tot[:, :MLSTM_DV]
            den = tot[:, MLSTM_DV:]
            hid = num / jnp.maximum(jnp.abs(den), jnp.exp(-m_row))

            m_new = jnp.maximum(b_last + m_prev, m_loc)
            a = jnp.exp(b_last + m_prev - m_new)
            cc = jnp.exp(m_loc - m_new)
            st_ref[h] = a * st + cc * _dot_tn((w_c * kf).astype(BF16), v_aug)
            m_ref[h:h + 1, :] = jnp.broadcast_to(m_new, (1, m_ref.shape[1]))

            ms = jnp.mean(hid * hid, axis=-1, keepdims=True)
            y = hid * lax.rsqrt(ms + EPS) * ng
            out_ref[rows, cv] = (y * _sigmoid(lo_ref[rows, cv])).astype(out_ref.dtype)
        return carry

    lax.fori_loop(0, tb // L, body, 0)


def _mlstm(proj, small, conv_w, gate_bias, norm_g, tril, bsz, t, tb):
    n = proj.shape[0]
    nt = t // tb

    def col(cidx):
        return pl.BlockSpec((tb, MLSTM_V), lambda b, s: (b * nt + s, cidx))

    const = lambda shape: pl.BlockSpec(shape, lambda b, s: (0,) * len(shape))
    return pl.pallas_call(
        _mlstm_kernel,
        grid=(bsz, nt),
        in_specs=[
            col(COL_ML_QK), col(COL_ML_V), col(COL_ML_O),
            pl.BlockSpec((tb, SMALL_W), lambda b, s: (b * nt + s, 0)),
            const((MLSTM_CONV, 2 * MLSTM_QK)), const((1, SMALL_W)), const((1, MLSTM_DV)),
            const((MLSTM_CHUNK, MLSTM_CHUNK)),
        ],
        out_specs=pl.BlockSpec((tb, MLSTM_V), lambda b, s: (b * nt + s, 0)),
        out_shape=jax.ShapeDtypeStruct((n, MLSTM_V), BF16),
        scratch_shapes=[
            pltpu.VMEM((tb + 8, 2 * MLSTM_QK), F32),
            pltpu.VMEM((tb, 2 * MLSTM_QK), F32),
            pltpu.VMEM((tb, SMALL_W), F32),
            pltpu.VMEM((MLSTM_HEADS, MLSTM_DQK, 2 * MLSTM_DV), F32),
            pltpu.VMEM((8, 128), F32),
        ],
        compiler_params=_params("parallel", "arbitrary"),
        name="mlstm",
    )(proj, proj, proj, small, conv_w, gate_bias, norm_g, tril)


def _moba_prep_kernel(q_ref, k_ref, v_ref, cos_ref, sin_ref, qg_ref, kg_ref, gm_ref,
                      qo_ref, ko_ref, vt_ref, km_ref):
    cos = cos_ref[...]
    sin = sin_ref[...]
    gm = gm_ref[...]
    lane = lax.broadcasted_iota(jnp.int32, cos.shape, 1)
    first_half = (lane % MOBA_HD) < (MOBA_HD // 2)

    def norm_rope(x, g):
        ms = _dot(x * x, gm, precision=HI)
        y = x * lax.rsqrt(ms + EPS) * g
        partner = jnp.where(first_half, pltpu.roll(y, 128 - MOBA_HD // 2, 1),
                            pltpu.roll(y, MOBA_HD // 2, 1))
        return y * cos + partner * sin

    for p in range(MOBA_W // 128):
        cs = slice(p * 128, (p + 1) * 128)
        qn = norm_rope(q_ref[:, cs], qg_ref[...])
        qo_ref[:, cs] = (qn * (MOBA_HD ** -0.5)).astype(BF16)
        kn = norm_rope(k_ref[:, cs], kg_ref[...])
        ko_ref[0, :, cs] = kn.astype(BF16)
        km_ref[0, :, cs] = jnp.mean(kn, axis=0, keepdims=True)
        vt_ref[0, cs, :] = v_ref[:, cs].T.astype(BF16)


def _moba_prep(proj, cos_t, sin_t, qg, kg, gmean, t):
    n = proj.shape[0]
    nblk = n // MOBA_BLOCK
    tblk = t // MOBA_BLOCK

    def col(cidx):
        return pl.BlockSpec((MOBA_BLOCK, MOBA_W), lambda i: (i, cidx))

    const = lambda shape: pl.BlockSpec(shape, lambda i: (0,) * len(shape))
    tab = pl.BlockSpec((MOBA_BLOCK, 128), lambda i: (i % tblk, 0))
    return pl.pallas_call(
        _moba_prep_kernel,
        grid=(nblk,),
        in_specs=[col(COL_MOBA_Q), col(COL_MOBA_K), col(COL_MOBA_V), tab, tab,
                  const((1, 128)), const((1, 128)), const((128, 128))],
        out_specs=[
            pl.BlockSpec((MOBA_BLOCK, MOBA_W), lambda i: (i, 0)),
            pl.BlockSpec((1, MOBA_BLOCK, MOBA_W), lambda i: (i, 0, 0)),
            pl.BlockSpec((1, MOBA_W, MOBA_BLOCK), lambda i: (i, 0, 0)),
            pl.BlockSpec((1, 1, MOBA_W), lambda i: (i, 0, 0)),
        ],
        out_shape=[
            jax.ShapeDtypeStruct((n, MOBA_W), BF16),
            jax.ShapeDtypeStruct((nblk, MOBA_BLOCK, MOBA_W), BF16),
            jax.ShapeDtypeStruct((nblk, MOBA_W, MOBA_BLOCK), BF16),
            jax.ShapeDtypeStruct((nblk, 1, MOBA_W), F32),
        ],
        compiler_params=_params("parallel"),
        name="moba_prep",
    )(proj, proj, proj, cos_t, sin_t, qg, kg, gmean)


def _moba_attn_kernel(q_ref, k_ref, vt_ref, km_ref, o_ref, sel_ref, m_ref, l_ref, acc_ref):
    i = pl.program_id(2)
    nb = k_ref.shape[0]
    B = MOBA_BLOCK
    q2 = q_ref[...]
    lane = lax.broadcasted_iota(jnp.int32, (1, 128), 1)
    kmb = km_ref[...].astype(BF16)
    blk = lax.broadcasted_iota(jnp.int32, (nb, B), 0)
    key_pos = lax.broadcasted_iota(jnp.int32, (B, B), 0)
    qry_pos = lax.broadcasted_iota(jnp.int32, (B, B), 1)
    neg = -jnp.inf

    qh = [jnp.where((lane // MOBA_HD) == h, q2, jnp.zeros_like(q2)) for h in range(2)]

    for h in range(2):
        g = jnp.where(blk < i, _dot_nt(kmb, qh[h]), neg)
        sel = jnp.zeros((nb, B), F32)
        for _ in range(MOBA_TOPK):
            mx = jnp.max(g, axis=0, keepdims=True)
            first = jnp.min(jnp.where(g == mx, blk, nb), axis=0, keepdims=True)
            pick = (blk == first) & (mx > neg)
            sel = jnp.where(pick, 1.0, sel)
            g = jnp.where(pick, neg, g)
        sel_ref[h] = sel

        s = jnp.where(key_pos <= qry_pos, _dot_nt(k_ref[i], qh[h]), neg)
        m = jnp.max(s, axis=0, keepdims=True)
        p = jnp.exp(s - m)
        m_ref[h] = m
        l_ref[h] = jnp.sum(p, axis=0, keepdims=True)
        acc_ref[h] = _dot(vt_ref[i, h * MOBA_HD:(h + 1) * MOBA_HD, :], p.astype(BF16))

    def body(j, carry):
        kj = k_ref[j]
        for h in range(2):
            s = jnp.where(sel_ref[h, pl.ds(j, 1), :] > 0.0, _dot_nt(kj, qh[h]), neg)
            m_old = m_ref[h]
            m_new = jnp.maximum(m_old, jnp.max(s, axis=0, keepdims=True))
            alpha = jnp.exp(m_old - m_new)
            p = jnp.exp(s - m_new)
            m_ref[h] = m_new
            l_ref[h] = alpha * l_ref[h] + jnp.sum(p, axis=0, keepdims=True)
            acc_ref[h] = alpha * acc_ref[h] + _dot(
                vt_ref[j, h * MOBA_HD:(h + 1) * MOBA_HD, :], p.astype(BF16))
        return carry

    lax.fori_loop(0, i, body, 0)

    out_t = jnp.concatenate([acc_ref[h] / l_ref[h] for h in range(2)], axis=0)
    o_ref[...] = out_t.T.astype(o_ref.dtype)


def _moba_attn(qn, kn, vt, km, bsz, t):
    n = qn.shape[0]
    nb = t // MOBA_BLOCK
    npair = MOBA_W // 128
    return pl.pallas_call(
        _moba_attn_kernel,
        grid=(bsz, npair, nb),
        in_specs=[
            pl.BlockSpec((MOBA_BLOCK, 128), lambda b, p, i: (b * nb + i, p)),
            pl.BlockSpec((nb, MOBA_BLOCK, 128), lambda b, p, i: (b, 0, p)),
            pl.BlockSpec((nb, 128, MOBA_BLOCK), lambda b, p, i: (b, p, 0)),
            pl.BlockSpec((None, nb, 128), lambda b, p, i: (b, 0, p)),
        ],
        out_specs=pl.BlockSpec((MOBA_BLOCK, 128), lambda b, p, i: (b * nb + i, p)),
        out_shape=jax.ShapeDtypeStruct((n, MOBA_W), BF16),
        scratch_shapes=[
            pltpu.VMEM((2, nb, MOBA_BLOCK), F32),
            pltpu.VMEM((2, 1, MOBA_BLOCK), F32),
            pltpu.VMEM((2, 1, MOBA_BLOCK), F32),
            pltpu.VMEM((2, MOBA_HD, MOBA_BLOCK), F32),
        ],
        compiler_params=_params("parallel", "parallel", "arbitrary"),
        name="moba_attn",
    )(qn, kn, vt, km.reshape(bsz, nb, MOBA_W))


def _merge_kernel(x_ref, yg_ref, ym_ref, yl_ref, g0_ref, g1_ref, g2_ref, gb_ref,
                  wg_ref, wm_ref, wl_ref, wo_ref, o_ref):
    gb = gb_ref[...]
    mixed = (_sigmoid(g0_ref[...] + gb[:, 0:D_MODEL]) * _dot(yg_ref[...], wg_ref[...])
             + _sigmoid(g1_ref[...] + gb[:, D_MODEL:2 * D_MODEL]) * _dot(ym_ref[...], wm_ref[...])
             + _sigmoid(g2_ref[...] + gb[:, 2 * D_MODEL:]) * _dot(yl_ref[...], wl_ref[...]))
    o_ref[...] = x_ref[...] + _dot(mixed.astype(BF16), wo_ref[...])


def _merge(x, y_gla, y_moba, y_ml, proj, gate_b, w_g, w_m, w_l, w_o, tm):
    n = x.shape[0]
    row = lambda w: pl.BlockSpec((tm, w), lambda i: (i, 0))
    const = lambda shape: pl.BlockSpec(shape, lambda i: (0,) * len(shape))
    gate = lambda c: pl.BlockSpec((tm, D_MODEL), lambda i: (i, COL_GATE0 + c))
    return pl.pallas_call(
        _merge_kernel,
        grid=(n // tm,),
        in_specs=[row(D_MODEL), row(GLA_W), row(MOBA_W), row(MLSTM_V),
                  gate(0), gate(1), gate(2), const((1, N_BRANCH * D_MODEL)),
                  const((GLA_W, D_MODEL)), const((MOBA_W, D_MODEL)), const((MLSTM_V, D_MODEL)),
                  const((D_MODEL, D_MODEL))],
        out_specs=row(D_MODEL),
        out_shape=jax.ShapeDtypeStruct((n, D_MODEL), F32),
        compiler_params=_params("parallel"),
        name="merge",
    )(x, y_gla, y_moba, y_ml, proj, proj, proj, gate_b, w_g, w_m, w_l, w_o)


def _ffn_kernel(x_ref, g_ref, w1_ref, w2_ref, o_ref, h_ref):
    @pl.when(pl.program_id(1) == 0)
    def _():
        x = x_ref[...]
        ms = jnp.mean(x * x, axis=-1, keepdims=True)
        h_ref[...] = (x * lax.rsqrt(ms + EPS) * g_ref[...]).astype(BF16)
        o_ref[...] = x

    a = jnp.square(jnp.maximum(_dot(h_ref[...], w1_ref[...]), 0.0)).astype(BF16)
    o_ref[...] += _dot(a, w2_ref[...])


def _ffn(x, g, w1, w2, tm, tf):
    n = x.shape[0]
    return pl.pallas_call(
        _ffn_kernel,
        grid=(n // tm, D_FF // tf),
        in_specs=[
            pl.BlockSpec((tm, D_MODEL), lambda i, j: (i, 0)),
            pl.BlockSpec((1, D_MODEL), lambda i, j: (0, 0)),
            pl.BlockSpec((D_MODEL, tf), lambda i, j: (0, j)),
            pl.BlockSpec((tf, D_MODEL), lambda i, j: (j, 0)),
        ],
        out_specs=pl.BlockSpec((tm, D_MODEL), lambda i, j: (i, 0)),
        out_shape=jax.ShapeDtypeStruct((n, D_MODEL), F32),
        scratch_shapes=[pltpu.VMEM((tm, D_MODEL), BF16)],
        compiler_params=_params("parallel", "arbitrary"),
        name="ffn",
    )(x, g, w1, w2)


def _rope_tables(seq):
    half = MOBA_HD // 2
    inv = 1.0 / (ROPE_THETA ** (jnp.arange(0, MOBA_HD, 2, dtype=F32) / MOBA_HD))
    ang = jnp.arange(seq, dtype=F32)[:, None] * inv[None, :]
    cos, sin = jnp.cos(ang), jnp.sin(ang)
    cos_t = jnp.tile(cos, (1, 128 // half))
    sin_t = jnp.tile(jnp.concatenate([-sin, sin], axis=1), (1, 128 // MOBA_HD))
    return cos_t, sin_t


def kernel(x, norm1_g, w_in, gla_a_up, gla_a_b, gla_norm_g, moba_qn_g, moba_kn_g, mlstm_conv_w,
           mlstm_i_b, mlstm_f_b, mlstm_norm_g, gate_b, w_br_gla, w_br_moba, w_br_mlstm, w_out,
           norm2_g, w_ff1, w_ff2):
    bsz, t, d = x.shape
    depth = w_in.shape[0]
    n = bsz * t
    assert d == D_MODEL and t % MOBA_BLOCK == 0
    tm = min(512, n)
    tb = min(512, t)

    o_ga = 4 * GLA_W
    o_moba = o_ga + GLA_LOWRANK
    o_li = o_moba + 3 * MOBA_W + 2 * MLSTM_QK + 2 * MLSTM_V
    o_gate = o_li + 2 * MLSTM_HEADS
    w_main = jnp.concatenate([w_in[:, :, :o_ga], w_in[:, :, o_moba:o_li], w_in[:, :, o_gate:]],
                             axis=2).astype(BF16)
    w_small = jnp.concatenate(
        [w_in[:, :, o_ga:o_moba], w_in[:, :, o_li:o_gate],
         jnp.zeros((depth, D_MODEL, SMALL_W - GLA_LOWRANK - 2 * MLSTM_HEADS), F32)],
        axis=2).astype(BF16)
    a_up = jnp.concatenate(
        [gla_a_up, jnp.zeros((depth, SMALL_W - GLA_LOWRANK, GLA_W), F32)], axis=1)
    ml_gate_b = jnp.concatenate(
        [jnp.zeros((depth, SM_LI), F32), mlstm_i_b, mlstm_f_b,
         jnp.zeros((depth, SMALL_W - SM_LF - MLSTM_HEADS), F32)], axis=1)
    qn_g = jnp.tile(moba_qn_g, (1, 128 // MOBA_HD))
    kn_g = jnp.tile(moba_kn_g, (1, 128 // MOBA_HD))
    w_g, w_m, w_l, w_o = (w.astype(BF16) for w in (w_br_gla, w_br_moba, w_br_mlstm, w_out))
    w1, w2 = w_ff1.astype(BF16), w_ff2.astype(BF16)

    cos_t, sin_t = _rope_tables(t)
    grp = jnp.arange(128) // MOBA_HD
    gmean = (grp[:, None] == grp[None, :]).astype(F32) / MOBA_HD
    tril = jnp.tril(jnp.ones((GLA_CHUNK, GLA_CHUNK), F32))

    xf = x.reshape(n, d)
    for l in range(depth):
        proj, small = _inproj(xf, norm1_g[l][None], w_main[l], w_small[l], tm, 1024)
        y_gla = _gla(proj, small, a_up[l], gla_a_b[l][None], gla_norm_g[l][None], tril, bsz, t, tb)
        y_ml = _mlstm(proj, small, mlstm_conv_w[l], ml_gate_b[l][None], mlstm_norm_g[l][None],
                      tril, bsz, t, tb)
        qn, kn, vt, km = _moba_prep(proj, cos_t, sin_t, qn_g[l][None], kn_g[l][None], gmean, t)
        y_moba = _moba_attn(qn, kn, vt, km, bsz, t)
        xf = _merge(xf, y_gla, y_moba, y_ml, proj, gate_b[l][None], w_g[l], w_m[l], w_l[l],
                    w_o[l], tm)
        xf = _ffn(xf, norm2_g[l][None], w1[l], w2[l], tm, 1024)
    return xf.reshape(bsz, t, d)
```

```python
import functools

import jax
import jax.numpy as jnp
from jax import lax
from jax.experimental import pallas as pl
from jax.experimental.pallas import tpu as pltpu

F32 = jnp.float32
BF16 = jnp.bfloat16
HI = lax.Precision.HIGHEST

D_MODEL = 1024
GLA_HEADS, GLA_DK, GLA_DV, GLA_LOWRANK = 4, 128, 128, 16
GLA_GATE_NORM = 16.0
GLA_CHUNK = 64
MOBA_HEADS, MOBA_HD, MOBA_BLOCK, MOBA_TOPK = 8, 64, 256, 3
ROPE_THETA = 10000.0
MLSTM_HEADS, MLSTM_DQK, MLSTM_DV = 4, 64, 128
MLSTM_CHUNK = 64
MLSTM_CONV = 4
D_FF = 4 * D_MODEL
N_BRANCH = 3
EPS = 1e-6

GLA_W = GLA_HEADS * GLA_DK
MOBA_W = MOBA_HEADS * MOBA_HD
MLSTM_QK = MLSTM_HEADS * MLSTM_DQK
MLSTM_V = MLSTM_HEADS * MLSTM_DV
MOBA_QSCALE = MOBA_HD ** -0.5 * 1.4426950408889634
MOBA_MASKED = -1e30

COL_GLA_Q, COL_GLA_K, COL_GLA_V, COL_GLA_G = 0, 1, 2, 3
COL_MOBA_Q, COL_MOBA_K, COL_MOBA_V = 4, 5, 6
COL_ML_QK, COL_ML_V, COL_ML_O = 7, 8, 9
COL_GATE0 = 5
N_MAIN = 8192
SM_GA, SM_LI, SM_LF = 0, 16, 20
SMALL_W = 128

VMEM_LIMIT = 48 * 1024 * 1024


def _dot(a, b, precision=None):
    return jnp.dot(a, b, preferred_element_type=F32, precision=precision)


def _dot_nt(a, b):
    return lax.dot_general(a, b, (((1,), (1,)), ((), ())), preferred_element_type=F32)


def _dot_tn(a, b):
    return lax.dot_general(a, b, (((0,), (0,)), ((), ())), preferred_element_type=F32)


def _sigmoid(x):
    return 1.0 / (1.0 + jnp.exp(-x))


def _log_sigmoid(x):
    return jnp.minimum(x, 0.0) - jnp.log(1.0 + jnp.exp(-jnp.abs(x)))


def _params(*sem):
    return pltpu.CompilerParams(dimension_semantics=sem, vmem_limit_bytes=VMEM_LIMIT)


def _inproj_kernel(x_ref, g_ref, w_ref, ws_ref, out_ref, small_ref, hn_ref):
    @pl.when(pl.program_id(1) == 0)
    def _():
        x = x_ref[...]
        ms = jnp.mean(x * x, axis=-1, keepdims=True)
        hn = (x * lax.rsqrt(ms + EPS) * g_ref[...]).astype(BF16)
        hn_ref[...] = hn
        small_ref[...] = _dot(hn, ws_ref[...])

    out_ref[...] = _dot(hn_ref[...], w_ref[...])


def _inproj(x, g, w_main, w_small, tm, tn):
    n = x.shape[0]
    return pl.pallas_call(
        _inproj_kernel,
        grid=(n // tm, N_MAIN // tn),
        in_specs=[
            pl.BlockSpec((tm, D_MODEL), lambda i, j: (i, 0)),
            pl.BlockSpec((1, D_MODEL), lambda i, j: (0, 0)),
            pl.BlockSpec((D_MODEL, tn), lambda i, j: (0, j)),
            pl.BlockSpec((D_MODEL, SMALL_W), lambda i, j: (0, 0)),
        ],
        out_specs=[
            pl.BlockSpec((tm, tn), lambda i, j: (i, j)),
            pl.BlockSpec((tm, SMALL_W), lambda i, j: (i, 0)),
        ],
        out_shape=[
            jax.ShapeDtypeStruct((n, N_MAIN), F32),
            jax.ShapeDtypeStruct((n, SMALL_W), F32),
        ],
        scratch_shapes=[pltpu.VMEM((tm, D_MODEL), BF16)],
        compiler_params=_params("parallel", "arbitrary"),
        name="inproj",
    )(x, g, w_main, w_small)


def _gla_kernel(q_ref, k_ref, v_ref, g_ref, small_ref, aup_ref, ab_ref, ng_ref, tril_ref,
                o_ref, st_ref, la_ref):
    tb = q_ref.shape[0]
    L = GLA_CHUNK

    @pl.when(pl.program_id(1) == 0)
    def _():
        st_ref[...] = jnp.zeros_like(st_ref)

    z = _dot(small_ref[...], aup_ref[...], precision=HI) + ab_ref[...]
    la_ref[...] = _log_sigmoid(z) * (1.0 / GLA_GATE_NORM)

    tril = tril_ref[...]
    causal = (lax.broadcasted_iota(jnp.int32, (L, L), 0)
              >= lax.broadcasted_iota(jnp.int32, (L, L), 1))
    scale = GLA_DK ** -0.5
    ng = ng_ref[...]

    def body(c, carry):
        rows = pl.ds(pl.multiple_of(c * L, L), L)
        for h in range(GLA_HEADS):
            cq = slice(h * GLA_DK, (h + 1) * GLA_DK)
            cv = slice(h * GLA_DV, (h + 1) * GLA_DV)
            bcum = _dot(tril, la_ref[rows, cq], precision=HI)
            b_last = bcum[L - 1:L, :]
            q = q_ref[rows, cq]
            k = k_ref[rows, cq]
            v = v_ref[rows, cv].astype(BF16)
            q_dec = (q * scale * jnp.exp(bcum)).astype(BF16)
            k_inv = (k * jnp.exp(-bcum)).astype(BF16)
            k_end = (k * jnp.exp(b_last - bcum)).astype(BF16)
            attn = jnp.where(causal, _dot_nt(q_dec, k_inv), 0.0).astype(BF16)
            st = st_ref[h]
            o = _dot(attn, v) + _dot_nt(q_dec, st.astype(BF16))
            st_ref[h] = st * jnp.exp(b_last) + _dot_tn(v, k_end)
            ms = jnp.mean(o * o, axis=-1, keepdims=True)
            y = o * lax.rsqrt(ms + EPS) * ng
            gg = g_ref[rows, cv]
            o_ref[rows, cv] = (y * (gg * _sigmoid(gg))).astype(o_ref.dtype)
        return carry

    lax.fori_loop(0, tb // L, body, 0)


def _gla(proj, small, a_up, a_b, norm_g, tril, bsz, t, tb):
    n = proj.shape[0]
    nt = t // tb

    def col(cidx):
        return pl.BlockSpec((tb, GLA_W), lambda b, s: (b * nt + s, cidx))

    const = lambda shape: pl.BlockSpec(shape, lambda b, s: (0,) * len(shape))
    return pl.pallas_call(
        _gla_kernel,
        grid=(bsz, nt),
        in_specs=[
            col(COL_GLA_Q), col(COL_GLA_K), col(COL_GLA_V), col(COL_GLA_G),
            pl.BlockSpec((tb, SMALL_W), lambda b, s: (b * nt + s, 0)),
            const((SMALL_W, GLA_W)), const((1, GLA_W)), const((1, GLA_DV)),
            const((GLA_CHUNK, GLA_CHUNK)),
        ],
        out_specs=pl.BlockSpec((tb, GLA_W), lambda b, s: (b * nt + s, 0)),
        out_shape=jax.ShapeDtypeStruct((n, GLA_W), BF16),
        scratch_shapes=[
            pltpu.VMEM((GLA_HEADS, GLA_DV, GLA_DK), F32),
            pltpu.VMEM((tb, GLA_W), F32),
        ],
        compiler_params=_params("parallel", "arbitrary"),
        name="gla",
    )(proj, proj, proj, proj, small, a_up, a_b, norm_g, tril)


def _mlstm_kernel(qk_ref, v_ref, lo_ref, small_ref, cw_ref, gb_ref, ng_ref, tril_ref,
                  out_ref, ext_ref, qkc_ref, gc_ref, st_ref, m_ref):
    tb = qk_ref.shape[0]
    L = MLSTM_CHUNK
    pad = 8

    @pl.when(pl.program_id(1) == 0)
    def _():
        st_ref[...] = jnp.zeros_like(st_ref)
        m_ref[...] = jnp.zeros_like(m_ref)
        ext_ref[0:pad, :] = jnp.zeros((pad, ext_ref.shape[1]), F32)

    x = qk_ref[...]
    ext_ref[pad:pad + tb, :] = x
    acc = None
    for w in range(MLSTM_CONV):
        off = pad - (MLSTM_CONV - 1) + w
        term = ext_ref[off:off + tb, :] * cw_ref[w:w + 1, :]
        acc = term if acc is None else acc + term
    qkc_ref[...] = acc
    ext_ref[0:pad, :] = x[tb - pad:tb, :]

    gsm = small_ref[...] + gb_ref[...]
    lane = lax.broadcasted_iota(jnp.int32, gsm.shape, 1)
    is_f = (lane >= SM_LF) & (lane < SM_LF + MLSTM_HEADS)
    gc_ref[...] = jnp.where(is_f, _log_sigmoid(gsm), gsm)

    tril = tril_ref[...]
    causal = (lax.broadcasted_iota(jnp.int32, (L, L), 0)
              >= lax.broadcasted_iota(jnp.int32, (L, L), 1))
    lane_c = lax.broadcasted_iota(jnp.int32, (L, SMALL_W), 1)
    is_f_c = (lane_c >= SM_LF) & (lane_c < SM_LF + MLSTM_HEADS)
    ones_aug = jnp.ones((L, MLSTM_DV), BF16)
    ng = ng_ref[...]
    kscale = MLSTM_DQK ** -0.5

    def body(c, carry):
        rows = pl.ds(pl.multiple_of(c * L, L), L)
        gates = gc_ref[rows, :]
        cum = _dot(tril, gates, precision=HI)
        pc = jnp.where(is_f_c, cum, gates)
        pr = pc.T
        for h in range(MLSTM_HEADS):
            bc = pc[:, SM_LF + h:SM_LF + h + 1]
            lic = pc[:, SM_LI + h:SM_LI + h + 1]
            br = pr[SM_LF + h:SM_LF + h + 1, :]
            lir = pr[SM_LI + h:SM_LI + h + 1, :]
            b_last = br[:, L - 1:L]
            log_d = jnp.where(causal, bc - br + lir, -jnp.inf)
            max_d = jnp.max(log_d, axis=-1, keepdims=True)
            m_loc = jnp.max(b_last - br + lir, axis=-1, keepdims=True)
            w_c = jnp.exp(b_last - bc + lic - m_loc)
            m_prev = m_ref[h:h + 1, 0:1]
            m_inter = bc + m_prev
            m_row = jnp.maximum(m_inter, max_d)

            q = qkc_ref[rows, h * MLSTM_DQK:(h + 1) * MLSTM_DQK].astype(BF16)
            kf = qkc_ref[rows, MLSTM_QK + h * MLSTM_DQK:MLSTM_QK + (h + 1) * MLSTM_DQK] * kscale
            cv = slice(h * MLSTM_DV, (h + 1) * MLSTM_DV)
            v_aug = jnp.concatenate([v_ref[rows, cv].astype(BF16), ones_aug], axis=1)

            s = _dot_nt(q, kf.astype(BF16)) * jnp.exp(log_d - m_row)
            inter_w = jnp.exp(m_inter - m_row)
            st = st_ref[h]
            tot = _dot(s.astype(BF16), v_aug) + inter_w * _dot(q, st.astype(BF16))
            num = tot[:, :MLSTM_DV]
            den = tot[:, MLSTM_DV:]
            hid = num / jnp.maximum(jnp.abs(den), jnp.exp(-m_row))

            m_new = jnp.maximum(b_last + m_prev, m_loc)
            a = jnp.exp(b_last + m_prev - m_new)
            cc = jnp.exp(m_loc - m_new)
            st_ref[h] = a * st + cc * _dot_tn((w_c * kf).astype(BF16), v_aug)
            m_ref[h:h + 1, :] = jnp.broadcast_to(m_new, (1, m_ref.shape[1]))

            ms = jnp.mean(hid * hid, axis=-1, keepdims=True)
            y = hid * lax.rsqrt(ms + EPS) * ng
            out_ref[rows, cv] = (y * _sigmoid(lo_ref[rows, cv])).astype(out_ref.dtype)
        return carry

    lax.fori_loop(0, tb // L, body, 0)


def _mlstm(proj, small, conv_w, gate_bias, norm_g, tril, bsz, t, tb):
    n = proj.shape[0]
    nt = t // tb

    def col(cidx):
        return pl.BlockSpec((tb, MLSTM_V), lambda b, s: (b * nt + s, cidx))

    const = lambda shape: pl.BlockSpec(shape, lambda b, s: (0,) * len(shape))
    return pl.pallas_call(
        _mlstm_kernel,
        grid=(bsz, nt),
        in_specs=[
            col(COL_ML_QK), col(COL_ML_V), col(COL_ML_O),
            pl.BlockSpec((tb, SMALL_W), lambda b, s: (b * nt + s, 0)),
            const((MLSTM_CONV, 2 * MLSTM_QK)), const((1, SMALL_W)), const((1, MLSTM_DV)),
            const((MLSTM_CHUNK, MLSTM_CHUNK)),
        ],
        out_specs=pl.BlockSpec((tb, MLSTM_V), lambda b, s: (b * nt + s, 0)),
        out_shape=jax.ShapeDtypeStruct((n, MLSTM_V), BF16),
        scratch_shapes=[
            pltpu.VMEM((tb + 8, 2 * MLSTM_QK), F32),
            pltpu.VMEM((tb, 2 * MLSTM_QK), F32),
            pltpu.VMEM((tb, SMALL_W), F32),
            pltpu.VMEM((MLSTM_HEADS, MLSTM_DQK, 2 * MLSTM_DV), F32),
            pltpu.VMEM((8, 128), F32),
        ],
        compiler_params=_params("parallel", "arbitrary"),
        name="mlstm",
    )(proj, proj, proj, small, conv_w, gate_bias, norm_g, tril)


def _moba_prep_kernel(q_ref, k_ref, v_ref, cos_ref, sin_ref, qg_ref, kg_ref, gm_ref,
                      qo_ref, ko_ref, vt_ref, km_ref, *, tblk):
    cos = cos_ref[...]
    sin = sin_ref[...]
    gm = gm_ref[...]
    lane = lax.broadcasted_iota(jnp.int32, cos.shape, 1)
    first_half = (lane % MOBA_HD) < (MOBA_HD // 2)
    low = lane < MOBA_HD
    j = pl.program_id(0) % tblk
    onehot = jnp.where(lane == MOBA_HD + j, 1.0, 0.0)

    def norm_rope(x, g):
        ms = _dot(x * x, gm, precision=HI)
        y = x * lax.rsqrt(ms + EPS) * g
        partner = jnp.where(first_half, pltpu.roll(y, 128 - MOBA_HD // 2, 1),
                            pltpu.roll(y, MOBA_HD // 2, 1))
        return y * cos + partner * sin

    for p in range(MOBA_W // 128):
        cs = slice(p * 128, (p + 1) * 128)
        qn = norm_rope(q_ref[:, cs], qg_ref[...]) * MOBA_QSCALE
        kn = norm_rope(k_ref[:, cs], kg_ref[...])
        for h in range(2):
            hh = 2 * p + h
            qh = qn if h == 0 else pltpu.roll(qn, MOBA_HD, 1)
            kh = kn if h == 0 else pltpu.roll(kn, MOBA_HD, 1)
            qo_ref[:, hh * 128:(hh + 1) * 128] = jnp.where(low, qh, 0.0).astype(BF16)
            ka = jnp.where(low, kh, onehot)
            ko_ref[0, hh] = ka.astype(BF16)
            km_ref[0, hh:hh + 1, :] = jnp.mean(ka, axis=0, keepdims=True)
        vt_ref[0, cs, :] = v_ref[:, cs].T.astype(BF16)


def _moba_prep(proj, cos_t, sin_t, qg, kg, gmean, t):
    n = proj.shape[0]
    nblk = n // MOBA_BLOCK
    tblk = t // MOBA_BLOCK

    def col(cidx):
        return pl.BlockSpec((MOBA_BLOCK, MOBA_W), lambda i: (i, cidx))

    const = lambda shape: pl.BlockSpec(shape, lambda i: (0,) * len(shape))
    tab = pl.BlockSpec((MOBA_BLOCK, 128), lambda i: (i % tblk, 0))
    return pl.pallas_call(
        functools.partial(_moba_prep_kernel, tblk=tblk),
        grid=(nblk,),
        in_specs=[col(COL_MOBA_Q), col(COL_MOBA_K), col(COL_MOBA_V), tab, tab,
                  const((1, 128)), const((1, 128)), const((128, 128))],
        out_specs=[
            pl.BlockSpec((MOBA_BLOCK, MOBA_HEADS * 128), lambda i: (i, 0)),
            pl.BlockSpec((1, MOBA_HEADS, MOBA_BLOCK, 128), lambda i: (i, 0, 0, 0)),
            pl.BlockSpec((1, MOBA_W, MOBA_BLOCK), lambda i: (i, 0, 0)),
            pl.BlockSpec((1, MOBA_HEADS, 128), lambda i: (i, 0, 0)),
        ],
        out_shape=[
            jax.ShapeDtypeStruct((n, MOBA_HEADS * 128), BF16),
            jax.ShapeDtypeStruct((nblk, MOBA_HEADS, MOBA_BLOCK, 128), BF16),
            jax.ShapeDtypeStruct((nblk, MOBA_W, MOBA_BLOCK), BF16),
            jax.ShapeDtypeStruct((nblk, MOBA_HEADS, 128), F32),
        ],
        compiler_params=_params("parallel"),
        name="moba_prep",
    )(proj, proj, proj, cos_t, sin_t, qg, kg, gmean)


def _moba_attn_kernel(q_ref, k_ref, vt_ref, km_ref, o_ref,
                      qa_ref, sa_ref, sb_ref, ma_ref, mb_ref, m_ref, l_ref, acc_ref):
    i = pl.program_id(2)
    nb = k_ref.shape[0]
    B = MOBA_BLOCK
    blk = lax.broadcasted_iota(jnp.int32, (nb, B), 0)
    key_pos = lax.broadcasted_iota(jnp.int32, (B, B), 0)
    qry_pos = lax.broadcasted_iota(jnp.int32, (B, B), 1)
    neg = -jnp.inf

    for h in range(2):
        qh = q_ref[:, h * 128:(h + 1) * 128]
        g = jnp.where(blk < i, _dot_nt(km_ref[h].astype(BF16), qh), neg)
        bias_t = jnp.full((nb, B), MOBA_MASKED, F32)
        for _ in range(MOBA_TOPK):
            mx = jnp.max(g, axis=0, keepdims=True)
            first = jnp.min(jnp.where(g == mx, blk, nb), axis=0, keepdims=True)
            pick = (blk == first) & (mx > neg)
            bias_t = jnp.where(pick, 0.0, bias_t)
            g = jnp.where(pick, neg, g)
        slab_t = jnp.concatenate(
            [jnp.zeros((MOBA_HD, B), F32), bias_t, jnp.zeros((128 - MOBA_HD - nb, B), F32)], axis=0)
        qa_ref[h] = (qh.astype(F32) + slab_t.T).astype(BF16)

        s = jnp.where(key_pos <= qry_pos, _dot_nt(k_ref[i, h], qh), neg)
        m = jnp.max(s, axis=0, keepdims=True)
        p = jnp.exp2(s - m)
        m_ref[h] = m
        l_ref[h] = jnp.sum(p, axis=0, keepdims=True)
        acc_ref[h] = _dot(vt_ref[i, h * MOBA_HD:(h + 1) * MOBA_HD, :], p.astype(BF16))

    def scores(j, s_ref, mx_ref):
        for h in range(2):
            s = _dot_nt(k_ref[j, h], qa_ref[h])
            s_ref[h] = s
            mx_ref[h] = jnp.max(s, axis=0, keepdims=True)

    def accumulate(j, s_ref, mx_ref):
        for h in range(2):
            m_old = m_ref[h]
            m_new = jnp.maximum(m_old, mx_ref[h])
            alpha = jnp.exp2(m_old - m_new)
            p = jnp.exp2(s_ref[h] - m_new)
            m_ref[h] = m_new
            l_ref[h] = alpha * l_ref[h] + jnp.sum(p, axis=0, keepdims=True)
            acc_ref[h] = alpha * acc_ref[h] + _dot(
                vt_ref[j, h * MOBA_HD:(h + 1) * MOBA_HD, :], p.astype(BF16))

    scores(0, sa_ref, ma_ref)

    def body(t, carry):
        j0 = 2 * t
        scores(j0 + 1, sb_ref, mb_ref)
        accumulate(j0, sa_ref, ma_ref)
        scores(jnp.minimum(j0 + 2, i), sa_ref, ma_ref)
        accumulate(j0 + 1, sb_ref, mb_ref)
        return carry

    lax.fori_loop(0, lax.shift_right_logical(i + 1, 1), body, 0)

    out_t = jnp.concatenate([acc_ref[h] / l_ref[h] for h in range(2)], axis=0)
    o_ref[...] = out_t.T.astype(o_ref.dtype)


def _moba_attn(qn, kn, vt, km, bsz, t):
    n = qn.shape[0]
    nb = t // MOBA_BLOCK
    npair = MOBA_W // 128
    return pl.pallas_call(
        _moba_attn_kernel,
        grid=(bsz, npair, nb),
        in_specs=[
            pl.BlockSpec((MOBA_BLOCK, 256), lambda b, p, i: (b * nb + i, p)),
            pl.BlockSpec((nb, 2, MOBA_BLOCK, 128), lambda b, p, i: (b, p, 0, 0)),
            pl.BlockSpec((nb, 128, MOBA_BLOCK), lambda b, p, i: (b, p, 0)),
            pl.BlockSpec((2, nb, 128), lambda b, p, i: (p, b, 0)),
        ],
        out_specs=pl.BlockSpec((MOBA_BLOCK, 128), lambda b, p, i: (b * nb + i, p)),
        out_shape=jax.ShapeDtypeStruct((n, MOBA_W), BF16),
        scratch_shapes=[
            pltpu.VMEM((2, MOBA_BLOCK, 128), BF16),
            pltpu.VMEM((2, MOBA_BLOCK, MOBA_BLOCK), F32),
            pltpu.VMEM((2, MOBA_BLOCK, MOBA_BLOCK), F32),
            pltpu.VMEM((2, 1, MOBA_BLOCK), F32),
            pltpu.VMEM((2, 1, MOBA_BLOCK), F32),
            pltpu.VMEM((2, 1, MOBA_BLOCK), F32),
            pltpu.VMEM((2, 1, MOBA_BLOCK), F32),
            pltpu.VMEM((2, MOBA_HD, MOBA_BLOCK), F32),
        ],
        compiler_params=_params("parallel", "parallel", "arbitrary"),
        name="moba_attn",
    )(qn, kn, vt, jnp.transpose(km, (1, 0, 2)))


def _merge_kernel(x_ref, yg_ref, ym_ref, yl_ref, g0_ref, g1_ref, g2_ref, gb_ref,
                  wg_ref, wm_ref, wl_ref, wo_ref, o_ref):
    gb = gb_ref[...]
    mixed = (_sigmoid(g0_ref[...] + gb[:, 0:D_MODEL]) * _dot(yg_ref[...], wg_ref[...])
             + _sigmoid(g1_ref[...] + gb[:, D_MODEL:2 * D_MODEL]) * _dot(ym_ref[...], wm_ref[...])
             + _sigmoid(g2_ref[...] + gb[:, 2 * D_MODEL:]) * _dot(yl_ref[...], wl_ref[...]))
    o_ref[...] = x_ref[...] + _dot(mixed.astype(BF16), wo_ref[...])


def _merge(x, y_gla, y_moba, y_ml, proj, gate_b, w_g, w_m, w_l, w_o, tm):
    n = x.shape[0]
    row = lambda w: pl.BlockSpec((tm, w), lambda i: (i, 0))
    const = lambda shape: pl.BlockSpec(shape, lambda i: (0,) * len(shape))
    gate = lambda c: pl.BlockSpec((tm, D_MODEL), lambda i: (i, COL_GATE0 + c))
    return pl.pallas_call(
        _merge_kernel,
        grid=(n // tm,),
        in_specs=[row(D_MODEL), row(GLA_W), row(MOBA_W), row(MLSTM_V),
                  gate(0), gate(1), gate(2), const((1, N_BRANCH * D_MODEL)),
                  const((GLA_W, D_MODEL)), const((MOBA_W, D_MODEL)), const((MLSTM_V, D_MODEL)),
                  const((D_MODEL, D_MODEL))],
        out_specs=row(D_MODEL),
        out_shape=jax.ShapeDtypeStruct((n, D_MODEL), F32),
        compiler_params=_params("parallel"),
        name="merge",
    )(x, y_gla, y_moba, y_ml, proj, proj, proj, gate_b, w_g, w_m, w_l, w_o)


def _ffn_kernel(x_ref, g_ref, w1_ref, w2_ref, o_ref, h_ref):
    @pl.when(pl.program_id(1) == 0)
    def _():
        x = x_ref[...]
        ms = jnp.mean(x * x, axis=-1, keepdims=True)
        h_ref[...] = (x * lax.rsqrt(ms + EPS) * g_ref[...]).astype(BF16)
        o_ref[...] = x

    a = jnp.square(jnp.maximum(_dot(h_ref[...], w1_ref[...]), 0.0)).astype(BF16)
    o_ref[...] += _dot(a, w2_ref[...])


def _ffn(x, g, w1, w2, tm, tf):
    n = x.shape[0]
    return pl.pallas_call(
        _ffn_kernel,
        grid=(n // tm, D_FF // tf),
        in_specs=[
            pl.BlockSpec((tm, D_MODEL), lambda i, j: (i, 0)),
            pl.BlockSpec((1, D_MODEL), lambda i, j: (0, 0)),
            pl.BlockSpec((D_MODEL, tf), lambda i, j: (0, j)),
            pl.BlockSpec((tf, D_MODEL), lambda i, j: (j, 0)),
        ],
        out_specs=pl.BlockSpec((tm, D_MODEL), lambda i, j: (i, 0)),
        out_shape=jax.ShapeDtypeStruct((n, D_MODEL), F32),
        scratch_shapes=[pltpu.VMEM((tm, D_MODEL), BF16)],
        compiler_params=_params("parallel", "arbitrary"),
        name="ffn",
    )(x, g, w1, w2)


def _rope_tables(seq):
    half = MOBA_HD // 2
    inv = 1.0 / (ROPE_THETA ** (jnp.arange(0, MOBA_HD, 2, dtype=F32) / MOBA_HD))
    ang = jnp.arange(seq, dtype=F32)[:, None] * inv[None, :]
    cos, sin = jnp.cos(ang), jnp.sin(ang)
    cos_t = jnp.tile(cos, (1, 128 // half))
    sin_t = jnp.tile(jnp.concatenate([-sin, sin], axis=1), (1, 128 // MOBA_HD))
    return cos_t, sin_t


def kernel(x, norm1_g, w_in, gla_a_up, gla_a_b, gla_norm_g, moba_qn_g, moba_kn_g, mlstm_conv_w,
           mlstm_i_b, mlstm_f_b, mlstm_norm_g, gate_b, w_br_gla, w_br_moba, w_br_mlstm, w_out,
           norm2_g, w_ff1, w_ff2):
    bsz, t, d = x.shape
    depth = w_in.shape[0]
    n = bsz * t
    assert d == D_MODEL and t % MOBA_BLOCK == 0
    tm = min(512, n)
    tb = min(512, t)

    o_ga = 4 * GLA_W
    o_moba = o_ga + GLA_LOWRANK
    o_li = o_moba + 3 * MOBA_W + 2 * MLSTM_QK + 2 * MLSTM_V
    o_gate = o_li + 2 * MLSTM_HEADS
    w_main = jnp.concatenate([w_in[:, :, :o_ga], w_in[:, :, o_moba:o_li], w_in[:, :, o_gate:]],
                             axis=2).astype(BF16)
    w_small = jnp.concatenate(
        [w_in[:, :, o_ga:o_moba], w_in[:, :, o_li:o_gate],
         jnp.zeros((depth, D_MODEL, SMALL_W - GLA_LOWRANK - 2 * MLSTM_HEADS), F32)],
        axis=2).astype(BF16)
    a_up = jnp.concatenate(
        [gla_a_up, jnp.zeros((depth, SMALL_W - GLA_LOWRANK, GLA_W), F32)], axis=1)
    ml_gate_b = jnp.concatenate(
        [jnp.zeros((depth, SM_LI), F32), mlstm_i_b, mlstm_f_b,
         jnp.zeros((depth, SMALL_W - SM_LF - MLSTM_HEADS), F32)], axis=1)
    qn_g = jnp.tile(moba_qn_g, (1, 128 // MOBA_HD))
    kn_g = jnp.tile(moba_kn_g, (1, 128 // MOBA_HD))
    w_g, w_m, w_l, w_o = (w.astype(BF16) for w in (w_br_gla, w_br_moba, w_br_mlstm, w_out))
    w1, w2 = w_ff1.astype(BF16), w_ff2.astype(BF16)

    cos_t, sin_t = _rope_tables(t)
    grp = jnp.arange(128) // MOBA_HD
    gmean = (grp[:, None] == grp[None, :]).astype(F32) / MOBA_HD
    tril = jnp.tril(jnp.ones((GLA_CHUNK, GLA_CHUNK), F32))

    xf = x.reshape(n, d)
    for l in range(depth):
        proj, small = _inproj(xf, norm1_g[l][None], w_main[l], w_small[l], tm, 1024)
        y_gla = _gla(proj, small, a_up[l], gla_a_b[l][None], gla_norm_g[l][None], tril, bsz, t, tb)
        y_ml = _mlstm(proj, small, mlstm_conv_w[l], ml_gate_b[l][None], mlstm_norm_g[l][None],
                      tril, bsz, t, tb)
        qn, kn, vt, km = _moba_prep(proj, cos_t, sin_t, qn_g[l][None], kn_g[l][None], gmean, t)
        y_moba = _moba_attn(qn, kn, vt, km, bsz, t)
        xf = _merge(xf, y_gla, y_moba, y_ml, proj, gate_b[l][None], w_g[l], w_m[l], w_l[l],
                    w_o[l], tm)
        xf = _ffn(xf, norm2_g[l][None], w1[l], w2[l], tm, 1024)
    return xf.reshape(bsz, t, d)
```

```python
import functools

import jax
import jax.numpy as jnp
from jax import lax
from jax.experimental import pallas as pl
from jax.experimental.pallas import tpu as pltpu

F32 = jnp.float32
BF16 = jnp.bfloat16
HI = lax.Precision.HIGHEST

D_MODEL = 1024
GLA_HEADS, GLA_DK, GLA_DV, GLA_LOWRANK = 4, 128, 128, 16
GLA_GATE_NORM = 16.0
GLA_CHUNK = 64
MOBA_HEADS, MOBA_HD, MOBA_BLOCK, MOBA_TOPK = 8, 64, 256, 3
ROPE_THETA = 10000.0
MLSTM_HEADS, MLSTM_DQK, MLSTM_DV = 4, 64, 128
MLSTM_CHUNK = 64
MLSTM_CONV = 4
D_FF = 4 * D_MODEL
N_BRANCH = 3
EPS = 1e-6

GLA_W = GLA_HEADS * GLA_DK
MOBA_W = MOBA_HEADS * MOBA_HD
MLSTM_QK = MLSTM_HEADS * MLSTM_DQK
MLSTM_V = MLSTM_HEADS * MLSTM_DV
MOBA_QSCALE = MOBA_HD ** -0.5 * 1.4426950408889634
MOBA_MASKED = -1e30

COL_GLA_Q, COL_GLA_K, COL_GLA_V, COL_GLA_G = 0, 1, 2, 3
COL_MOBA_Q, COL_MOBA_K, COL_MOBA_V = 4, 5, 6
COL_ML_QK, COL_ML_V, COL_ML_O = 7, 8, 9
COL_GATE0 = 5
N_MAIN = 8192
SM_GA, SM_LI, SM_LF = 0, 16, 20
SMALL_W = 128

VMEM_LIMIT = 48 * 1024 * 1024


def _dot(a, b, precision=None):
    return jnp.dot(a, b, preferred_element_type=F32, precision=precision)


def _dot_nt(a, b):
    return lax.dot_general(a, b, (((1,), (1,)), ((), ())), preferred_element_type=F32)


def _dot_tn(a, b):
    return lax.dot_general(a, b, (((0,), (0,)), ((), ())), preferred_element_type=F32)


def _sigmoid(x):
    return 1.0 / (1.0 + jnp.exp(-x))


def _log_sigmoid(x):
    return jnp.minimum(x, 0.0) - jnp.log(1.0 + jnp.exp(-jnp.abs(x)))


def _params(*sem):
    return pltpu.CompilerParams(dimension_semantics=sem, vmem_limit_bytes=VMEM_LIMIT)


def _inproj_kernel(x_ref, g_ref, w_ref, ws_ref, out_ref, small_ref, hn_ref):
    @pl.when(pl.program_id(1) == 0)
    def _():
        x = x_ref[...]
        ms = jnp.mean(x * x, axis=-1, keepdims=True)
        hn = (x * lax.rsqrt(ms + EPS) * g_ref[...]).astype(BF16)
        hn_ref[...] = hn
        small_ref[...] = _dot(hn, ws_ref[...])

    out_ref[...] = _dot(hn_ref[...], w_ref[...])


def _inproj(x, g, w_main, w_small, tm, tn):
    n = x.shape[0]
    return pl.pallas_call(
        _inproj_kernel,
        grid=(n // tm, N_MAIN // tn),
        in_specs=[
            pl.BlockSpec((tm, D_MODEL), lambda i, j: (i, 0)),
            pl.BlockSpec((1, D_MODEL), lambda i, j: (0, 0)),
            pl.BlockSpec((D_MODEL, tn), lambda i, j: (0, j)),
            pl.BlockSpec((D_MODEL, SMALL_W), lambda i, j: (0, 0)),
        ],
        out_specs=[
            pl.BlockSpec((tm, tn), lambda i, j: (i, j)),
            pl.BlockSpec((tm, SMALL_W), lambda i, j: (i, 0)),
        ],
        out_shape=[
            jax.ShapeDtypeStruct((n, N_MAIN), F32),
            jax.ShapeDtypeStruct((n, SMALL_W), F32),
        ],
        scratch_shapes=[pltpu.VMEM((tm, D_MODEL), BF16)],
        compiler_params=_params("parallel", "arbitrary"),
        name="inproj",
    )(x, g, w_main, w_small)


def _gla_kernel(q_ref, k_ref, v_ref, g_ref, small_ref, aup_ref, ab_ref, ng_ref, tril_ref,
                o_ref, st_ref, la_ref):
    tb = q_ref.shape[0]
    L = GLA_CHUNK

    @pl.when(pl.program_id(1) == 0)
    def _():
        st_ref[...] = jnp.zeros_like(st_ref)

    z = _dot(small_ref[...], aup_ref[...], precision=HI) + ab_ref[...]
    la_ref[...] = _log_sigmoid(z) * (1.0 / GLA_GATE_NORM)

    tril = tril_ref[...]
    causal = (lax.broadcasted_iota(jnp.int32, (L, L), 0)
              >= lax.broadcasted_iota(jnp.int32, (L, L), 1))
    scale = GLA_DK ** -0.5
    ng = ng_ref[...]

    heads = range(GLA_HEADS)
    hq = [slice(h * GLA_DK, (h + 1) * GLA_DK) for h in heads]
    hv = [slice(h * GLA_DV, (h + 1) * GLA_DV) for h in heads]

    def body(c, carry):
        rows = pl.ds(pl.multiple_of(c * L, L), L)
        bcum = _dot(tril, la_ref[rows, :], precision=HI)
        b_last = bcum[L - 1:L, :]
        q = q_ref[rows, :]
        k = k_ref[rows, :]
        v = v_ref[rows, :].astype(BF16)
        q_dec = (q * scale * jnp.exp(bcum)).astype(BF16)
        k_inv = (k * jnp.exp(-bcum)).astype(BF16)
        k_end = (k * jnp.exp(b_last - bcum)).astype(BF16)
        decay = jnp.exp(b_last)
        st = [st_ref[h] for h in heads]
        attn = [_dot_nt(q_dec[:, hq[h]], k_inv[:, hq[h]]) for h in heads]
        o_inter = [_dot_nt(q_dec[:, hq[h]], st[h].astype(BF16)) for h in heads]
        kv = [_dot_tn(v[:, hv[h]], k_end[:, hq[h]]) for h in heads]
        attn = [jnp.where(causal, a, 0.0).astype(BF16) for a in attn]
        o = [_dot(attn[h], v[:, hv[h]]) + o_inter[h] for h in heads]
        for h in heads:
            st_ref[h] = st[h] * decay[:, hq[h]] + kv[h]
            ms = jnp.mean(o[h] * o[h], axis=-1, keepdims=True)
            y = o[h] * lax.rsqrt(ms + EPS) * ng
            gg = g_ref[rows, hv[h]]
            o_ref[rows, hv[h]] = (y * (gg * _sigmoid(gg))).astype(o_ref.dtype)
        return carry

    lax.fori_loop(0, tb // L, body, 0)


def _gla(proj, small, a_up, a_b, norm_g, tril, bsz, t, tb):
    n = proj.shape[0]
    nt = t // tb

    def col(cidx):
        return pl.BlockSpec((tb, GLA_W), lambda b, s: (b * nt + s, cidx))

    const = lambda shape: pl.BlockSpec(shape, lambda b, s: (0,) * len(shape))
    return pl.pallas_call(
        _gla_kernel,
        grid=(bsz, nt),
        in_specs=[
            col(COL_GLA_Q), col(COL_GLA_K), col(COL_GLA_V), col(COL_GLA_G),
            pl.BlockSpec((tb, SMALL_W), lambda b, s: (b * nt + s, 0)),
            const((SMALL_W, GLA_W)), const((1, GLA_W)), const((1, GLA_DV)),
            const((GLA_CHUNK, GLA_CHUNK)),
        ],
        out_specs=pl.BlockSpec((tb, GLA_W), lambda b, s: (b * nt + s, 0)),
        out_shape=jax.ShapeDtypeStruct((n, GLA_W), BF16),
        scratch_shapes=[
            pltpu.VMEM((GLA_HEADS, GLA_DV, GLA_DK), F32),
            pltpu.VMEM((tb, GLA_W), F32),
        ],
        compiler_params=_params("parallel", "arbitrary"),
        name="gla",
    )(proj, proj, proj, proj, small, a_up, a_b, norm_g, tril)


def _mlstm_kernel(qk_ref, v_ref, lo_ref, small_ref, cw_ref, gb_ref, ng_ref, tril_ref,
                  out_ref, ext_ref, qkc_ref, gc_ref, st_ref, m_ref):
    tb = qk_ref.shape[0]
    L = MLSTM_CHUNK
    pad = 8

    @pl.when(pl.program_id(1) == 0)
    def _():
        st_ref[...] = jnp.zeros_like(st_ref)
        m_ref[...] = jnp.zeros_like(m_ref)
        ext_ref[0:pad, :] = jnp.zeros((pad, ext_ref.shape[1]), F32)

    x = qk_ref[...]
    ext_ref[pad:pad + tb, :] = x
    acc = None
    for w in range(MLSTM_CONV):
        off = pad - (MLSTM_CONV - 1) + w
        term = ext_ref[off:off + tb, :] * cw_ref[w:w + 1, :]
        acc = term if acc is None else acc + term
    qkc_ref[...] = acc
    ext_ref[0:pad, :] = x[tb - pad:tb, :]

    gsm = small_ref[...] + gb_ref[...]
    lane = lax.broadcasted_iota(jnp.int32, gsm.shape, 1)
    is_f = (lane >= SM_LF) & (lane < SM_LF + MLSTM_HEADS)
    gc_ref[...] = jnp.where(is_f, _log_sigmoid(gsm), gsm)

    tril = tril_ref[...]
    causal = (lax.broadcasted_iota(jnp.int32, (L, L), 0)
              >= lax.broadcasted_iota(jnp.int32, (L, L), 1))
    lane_c = lax.broadcasted_iota(jnp.int32, (L, SMALL_W), 1)
    is_f_c = (lane_c >= SM_LF) & (lane_c < SM_LF + MLSTM_HEADS)
    ones_aug = jnp.ones((L, MLSTM_DV), BF16)
    ng = ng_ref[...]
    kscale = MLSTM_DQK ** -0.5

    def body(c, carry):
        rows = pl.ds(pl.multiple_of(c * L, L), L)
        gates = gc_ref[rows, :]
        cum = _dot(tril, gates, precision=HI)
        pc = jnp.where(is_f_c, cum, gates)
        pr = pc.T
        heads = range(MLSTM_HEADS)
        hv = [slice(h * MLSTM_DV, (h + 1) * MLSTM_DV) for h in heads]
        qk_all = qkc_ref[rows, :]
        v_all = v_ref[rows, :].astype(BF16)
        st = [st_ref[h] for h in heads]

        dmat, inter_w, m_row, w_c, m_new, a_scale, c_scale = [], [], [], [], [], [], []
        for h in heads:
            bc = pc[:, SM_LF + h:SM_LF + h + 1]
            lic = pc[:, SM_LI + h:SM_LI + h + 1]
            br = pr[SM_LF + h:SM_LF + h + 1, :]
            lir = pr[SM_LI + h:SM_LI + h + 1, :]
            b_last = br[:, L - 1:L]
            log_d = jnp.where(causal, bc - br + lir, -jnp.inf)
            max_d = jnp.max(log_d, axis=-1, keepdims=True)
            m_loc = jnp.max(b_last - br + lir, axis=-1, keepdims=True)
            w_c.append(jnp.exp(b_last - bc + lic - m_loc))
            m_prev = m_ref[h:h + 1, 0:1]
            m_inter = bc + m_prev
            m_row.append(jnp.maximum(m_inter, max_d))
            dmat.append(jnp.exp(log_d - m_row[h]))
            inter_w.append(jnp.exp(m_inter - m_row[h]))
            m_new.append(jnp.maximum(b_last + m_prev, m_loc))
            a_scale.append(jnp.exp(b_last + m_prev - m_new[h]))
            c_scale.append(jnp.exp(m_loc - m_new[h]))

        q = [qk_all[:, h * MLSTM_DQK:(h + 1) * MLSTM_DQK].astype(BF16) for h in heads]
        kf = [qk_all[:, MLSTM_QK + h * MLSTM_DQK:MLSTM_QK + (h + 1) * MLSTM_DQK] * kscale
              for h in heads]
        v_aug = [jnp.concatenate([v_all[:, hv[h]], ones_aug], axis=1) for h in heads]
        qk = [_dot_nt(q[h], kf[h].astype(BF16)) for h in heads]
        inter = [_dot(q[h], st[h].astype(BF16)) for h in heads]
        kv = [_dot_tn((w_c[h] * kf[h]).astype(BF16), v_aug[h]) for h in heads]

        s = [(qk[h] * dmat[h]).astype(BF16) for h in heads]
        intra = [_dot(s[h], v_aug[h]) for h in heads]
        for h in heads:
            tot = intra[h] + inter_w[h] * inter[h]
            num = tot[:, :MLSTM_DV]
            den = tot[:, MLSTM_DV:]
            hid = num / jnp.maximum(jnp.abs(den), jnp.exp(-m_row[h]))
            st_ref[h] = a_scale[h] * st[h] + c_scale[h] * kv[h]
            m_ref[h:h + 1, :] = jnp.broadcast_to(m_new[h], (1, m_ref.shape[1]))
            ms = jnp.mean(hid * hid, axis=-1, keepdims=True)
            y = hid * lax.rsqrt(ms + EPS) * ng
            out_ref[rows, hv[h]] = (y * _sigmoid(lo_ref[rows, hv[h]])).astype(out_ref.dtype)
        return carry

    lax.fori_loop(0, tb // L, body, 0)


def _mlstm(proj, small, conv_w, gate_bias, norm_g, tril, bsz, t, tb):
    n = proj.shape[0]
    nt = t // tb

    def col(cidx):
        return pl.BlockSpec((tb, MLSTM_V), lambda b, s: (b * nt + s, cidx))

    const = lambda shape: pl.BlockSpec(shape, lambda b, s: (0,) * len(shape))
    return pl.pallas_call(
        _mlstm_kernel,
        grid=(bsz, nt),
        in_specs=[
            col(COL_ML_QK), col(COL_ML_V), col(COL_ML_O),
            pl.BlockSpec((tb, SMALL_W), lambda b, s: (b * nt + s, 0)),
            const((MLSTM_CONV, 2 * MLSTM_QK)), const((1, SMALL_W)), const((1, MLSTM_DV)),
            const((MLSTM_CHUNK, MLSTM_CHUNK)),
        ],
        out_specs=pl.BlockSpec((tb, MLSTM_V), lambda b, s: (b * nt + s, 0)),
        out_shape=jax.ShapeDtypeStruct((n, MLSTM_V), BF16),
        scratch_shapes=[
            pltpu.VMEM((tb + 8, 2 * MLSTM_QK), F32),
            pltpu.VMEM((tb, 2 * MLSTM_QK), F32),
            pltpu.VMEM((tb, SMALL_W), F32),
            pltpu.VMEM((MLSTM_HEADS, MLSTM_DQK, 2 * MLSTM_DV), F32),
            pltpu.VMEM((8, 128), F32),
        ],
        compiler_params=_params("parallel", "arbitrary"),
        name="mlstm",
    )(proj, proj, proj, small, conv_w, gate_bias, norm_g, tril)


def _moba_prep_kernel(q_ref, k_ref, v_ref, cos_ref, sin_ref, qg_ref, kg_ref, gm_ref,
                      qo_ref, ko_ref, vt_ref, km_ref, *, tblk):
    cos = cos_ref[...]
    sin = sin_ref[...]
    gm = gm_ref[...]
    lane = lax.broadcasted_iota(jnp.int32, cos.shape, 1)
    first_half = (lane % MOBA_HD) < (MOBA_HD // 2)
    low = lane < MOBA_HD
    j = pl.program_id(0) % tblk
    onehot = jnp.where(lane == MOBA_HD + j, 1.0, 0.0)

    def norm_rope(x, g):
        ms = _dot(x * x, gm, precision=HI)
        y = x * lax.rsqrt(ms + EPS) * g
        partner = jnp.where(first_half, pltpu.roll(y, 128 - MOBA_HD // 2, 1),
                            pltpu.roll(y, MOBA_HD // 2, 1))
        return y * cos + partner * sin

    for p in range(MOBA_W // 128):
        cs = slice(p * 128, (p + 1) * 128)
        qn = norm_rope(q_ref[:, cs], qg_ref[...]) * MOBA_QSCALE
        kn = norm_rope(k_ref[:, cs], kg_ref[...])
        for h in range(2):
            hh = 2 * p + h
            qh = qn if h == 0 else pltpu.roll(qn, MOBA_HD, 1)
            kh = kn if h == 0 else pltpu.roll(kn, MOBA_HD, 1)
            qo_ref[:, hh * 128:(hh + 1) * 128] = jnp.where(low, qh, 0.0).astype(BF16)
            ka = jnp.where(low, kh, onehot)
            ko_ref[0, hh] = ka.astype(BF16)
            km_ref[0, hh:hh + 1, :] = jnp.mean(ka, axis=0, keepdims=True)
        vt_ref[0, cs, :] = v_ref[:, cs].T.astype(BF16)


def _moba_prep(proj, cos_t, sin_t, qg, kg, gmean, t):
    n = proj.shape[0]
    nblk = n // MOBA_BLOCK
    tblk = t // MOBA_BLOCK

    def col(cidx):
        return pl.BlockSpec((MOBA_BLOCK, MOBA_W), lambda i: (i, cidx))

    const = lambda shape: pl.BlockSpec(shape, lambda i: (0,) * len(shape))
    tab = pl.BlockSpec((MOBA_BLOCK, 128), lambda i: (i % tblk, 0))
    return pl.pallas_call(
        functools.partial(_moba_prep_kernel, tblk=tblk),
        grid=(nblk,),
        in_specs=[col(COL_MOBA_Q), col(COL_MOBA_K), col(COL_MOBA_V), tab, tab,
                  const((1, 128)), const((1, 128)), const((128, 128))],
        out_specs=[
            pl.BlockSpec((MOBA_BLOCK, MOBA_HEADS * 128), lambda i: (i, 0)),
            pl.BlockSpec((1, MOBA_HEADS, MOBA_BLOCK, 128), lambda i: (i, 0, 0, 0)),
            pl.BlockSpec((1, MOBA_W, MOBA_BLOCK), lambda i: (i, 0, 0)),
            pl.BlockSpec((1, MOBA_HEADS, 128), lambda i: (i, 0, 0)),
        ],
        out_shape=[
            jax.ShapeDtypeStruct((n, MOBA_HEADS * 128), BF16),
            jax.ShapeDtypeStruct((nblk, MOBA_HEADS, MOBA_BLOCK, 128), BF16),
            jax.ShapeDtypeStruct((nblk, MOBA_W, MOBA_BLOCK), BF16),
            jax.ShapeDtypeStruct((nblk, MOBA_HEADS, 128), F32),
        ],
        compiler_params=_params("parallel"),
        name="moba_prep",
    )(proj, proj, proj, cos_t, sin_t, qg, kg, gmean)


def _moba_attn_kernel(q_ref, k_ref, vt_ref, km_ref, o_ref,
                      qa_ref, sa_ref, sb_ref, sc_ref, sd_ref, ma_ref, mb_ref, mc_ref, md_ref,
                      m_ref, l_ref, acc_ref):
    i = pl.program_id(2)
    nb = k_ref.shape[0]
    B = MOBA_BLOCK
    blk = lax.broadcasted_iota(jnp.int32, (nb, B), 0)
    key_pos = lax.broadcasted_iota(jnp.int32, (B, B), 0)
    qry_pos = lax.broadcasted_iota(jnp.int32, (B, B), 1)
    neg = -jnp.inf

    for h in range(2):
        qh = q_ref[:, h * 128:(h + 1) * 128]
        g = jnp.where(blk < i, _dot_nt(km_ref[h].astype(BF16), qh), neg)
        bias_t = jnp.full((nb, B), MOBA_MASKED, F32)
        for _ in range(MOBA_TOPK):
            mx = jnp.max(g, axis=0, keepdims=True)
            first = jnp.min(jnp.where(g == mx, blk, nb), axis=0, keepdims=True)
            pick = (blk == first) & (mx > neg)
            bias_t = jnp.where(pick, 0.0, bias_t)
            g = jnp.where(pick, neg, g)
        slab_t = jnp.concatenate(
            [jnp.zeros((MOBA_HD, B), F32), bias_t, jnp.zeros((128 - MOBA_HD - nb, B), F32)], axis=0)
        qa_ref[h] = (qh.astype(F32) + slab_t.T).astype(BF16)
        m_ref[h] = jnp.full((1, B), neg, F32)
        l_ref[h] = jnp.zeros((1, B), F32)
        acc_ref[h] = jnp.zeros((MOBA_HD, B), F32)

    def stage(h, s, slot):
        s_ref, mx_ref = slot
        s_ref[h] = s
        mx_ref[h] = jnp.max(s, axis=0, keepdims=True)

    def scores_own(slot):
        for h in range(2):
            s = _dot_nt(k_ref[i, h], q_ref[:, h * 128:(h + 1) * 128])
            stage(h, jnp.where(key_pos <= qry_pos, s, neg), slot)

    def scores(j, slot):
        for h in range(2):
            stage(h, _dot_nt(k_ref[j, h], qa_ref[h]), slot)

    def accumulate(j, slot):
        s_ref, mx_ref = slot
        for h in range(2):
            m_old = m_ref[h]
            m_new = jnp.maximum(m_old, mx_ref[h])
            alpha = jnp.exp2(m_old - m_new)
            p = jnp.exp2(s_ref[h] - m_new)
            m_ref[h] = m_new
            l_ref[h] = alpha * l_ref[h] + jnp.sum(p, axis=0, keepdims=True)
            acc_ref[h] = alpha * acc_ref[h] + _dot(
                vt_ref[j, h * MOBA_HD:(h + 1) * MOBA_HD, :], p.astype(BF16))

    def block_at(n):
        return jnp.minimum(n - 1, i)

    slot_a, slot_b, slot_c, slot_d = ((sa_ref, ma_ref), (sb_ref, mb_ref),
                                      (sc_ref, mc_ref), (sd_ref, md_ref))
    scores_own(slot_a)
    scores(block_at(1), slot_b)

    def body(u, carry):
        n0 = 4 * u
        scores(block_at(n0 + 2), slot_c)
        accumulate(jnp.where(u == 0, i, block_at(n0)), slot_a)
        scores(block_at(n0 + 3), slot_d)
        accumulate(block_at(n0 + 1), slot_b)
        scores(block_at(n0 + 4), slot_a)
        accumulate(block_at(n0 + 2), slot_c)
        scores(block_at(n0 + 5), slot_b)
        accumulate(block_at(n0 + 3), slot_d)
        return carry

    lax.fori_loop(0, lax.shift_right_logical(i + 4, 2), body, 0)

    out_t = jnp.concatenate([acc_ref[h] / l_ref[h] for h in range(2)], axis=0)
    o_ref[...] = out_t.T.astype(o_ref.dtype)


def _moba_attn(qn, kn, vt, km, bsz, t):
    n = qn.shape[0]
    nb = t // MOBA_BLOCK
    npair = MOBA_W // 128
    return pl.pallas_call(
        _moba_attn_kernel,
        grid=(bsz, npair, nb),
        in_specs=[
            pl.BlockSpec((MOBA_BLOCK, 256), lambda b, p, i: (b * nb + i, p)),
            pl.BlockSpec((nb, 2, MOBA_BLOCK, 128), lambda b, p, i: (b, p, 0, 0)),
            pl.BlockSpec((nb, 128, MOBA_BLOCK), lambda b, p, i: (b, p, 0)),
            pl.BlockSpec((2, nb, 128), lambda b, p, i: (p, b, 0)),
        ],
        out_specs=pl.BlockSpec((MOBA_BLOCK, 128), lambda b, p, i: (b * nb + i, p)),
        out_shape=jax.ShapeDtypeStruct((n, MOBA_W), BF16),
        scratch_shapes=[
            pltpu.VMEM((2, MOBA_BLOCK, 128), BF16),
            pltpu.VMEM((2, MOBA_BLOCK, MOBA_BLOCK), F32),
            pltpu.VMEM((2, MOBA_BLOCK, MOBA_BLOCK), F32),
            pltpu.VMEM((2, MOBA_BLOCK, MOBA_BLOCK), F32),
            pltpu.VMEM((2, MOBA_BLOCK, MOBA_BLOCK), F32),
            pltpu.VMEM((2, 1, MOBA_BLOCK), F32),
            pltpu.VMEM((2, 1, MOBA_BLOCK), F32),
            pltpu.VMEM((2, 1, MOBA_BLOCK), F32),
            pltpu.VMEM((2, 1, MOBA_BLOCK), F32),
            pltpu.VMEM((2, 1, MOBA_BLOCK), F32),
            pltpu.VMEM((2, 1, MOBA_BLOCK), F32),
            pltpu.VMEM((2, MOBA_HD, MOBA_BLOCK), F32),
        ],
        compiler_params=_params("parallel", "parallel", "arbitrary"),
        name="moba_attn",
    )(qn, kn, vt, jnp.transpose(km, (1, 0, 2)))


def _merge_kernel(x_ref, yg_ref, ym_ref, yl_ref, g0_ref, g1_ref, g2_ref, gb_ref,
                  wg_ref, wm_ref, wl_ref, wo_ref, o_ref):
    gb = gb_ref[...]
    mixed = (_sigmoid(g0_ref[...] + gb[:, 0:D_MODEL]) * _dot(yg_ref[...], wg_ref[...])
             + _sigmoid(g1_ref[...] + gb[:, D_MODEL:2 * D_MODEL]) * _dot(ym_ref[...], wm_ref[...])
             + _sigmoid(g2_ref[...] + gb[:, 2 * D_MODEL:]) * _dot(yl_ref[...], wl_ref[...]))
    o_ref[...] = x_ref[...] + _dot(mixed.astype(BF16), wo_ref[...])


def _merge(x, y_gla, y_moba, y_ml, proj, gate_b, w_g, w_m, w_l, w_o, tm):
    n = x.shape[0]
    row = lambda w: pl.BlockSpec((tm, w), lambda i: (i, 0))
    const = lambda shape: pl.BlockSpec(shape, lambda i: (0,) * len(shape))
    gate = lambda c: pl.BlockSpec((tm, D_MODEL), lambda i: (i, COL_GATE0 + c))
    return pl.pallas_call(
        _merge_kernel,
        grid=(n // tm,),
        in_specs=[row(D_MODEL), row(GLA_W), row(MOBA_W), row(MLSTM_V),
                  gate(0), gate(1), gate(2), const((1, N_BRANCH * D_MODEL)),
                  const((GLA_W, D_MODEL)), const((MOBA_W, D_MODEL)), const((MLSTM_V, D_MODEL)),
                  const((D_MODEL, D_MODEL))],
        out_specs=row(D_MODEL),
        out_shape=jax.ShapeDtypeStruct((n, D_MODEL), F32),
        compiler_params=_params("parallel"),
        name="merge",
    )(x, y_gla, y_moba, y_ml, proj, proj, proj, gate_b, w_g, w_m, w_l, w_o)


def _ffn_kernel(x_ref, g_ref, w1_ref, w2_ref, o_ref, h_ref):
    @pl.when(pl.program_id(1) == 0)
    def _():
        x = x_ref[...]
        ms = jnp.mean(x * x, axis=-1, keepdims=True)
        h_ref[...] = (x * lax.rsqrt(ms + EPS) * g_ref[...]).astype(BF16)
        o_ref[...] = x

    a = jnp.square(jnp.maximum(_dot(h_ref[...], w1_ref[...]), 0.0)).astype(BF16)
    o_ref[...] += _dot(a, w2_ref[...])


def _ffn(x, g, w1, w2, tm, tf):
    n = x.shape[0]
    return pl.pallas_call(
        _ffn_kernel,
        grid=(n // tm, D_FF // tf),
        in_specs=[
            pl.BlockSpec((tm, D_MODEL), lambda i, j: (i, 0)),
            pl.BlockSpec((1, D_MODEL), lambda i, j: (0, 0)),
            pl.BlockSpec((D_MODEL, tf), lambda i, j: (0, j)),
            pl.BlockSpec((tf, D_MODEL), lambda i, j: (j, 0)),
        ],
        out_specs=pl.BlockSpec((tm, D_MODEL), lambda i, j: (i, 0)),
        out_shape=jax.ShapeDtypeStruct((n, D_MODEL), F32),
        scratch_shapes=[pltpu.VMEM((tm, D_MODEL), BF16)],
        compiler_params=_params("parallel", "arbitrary"),
        name="ffn",
    )(x, g, w1, w2)


def _rope_tables(seq):
    half = MOBA_HD // 2
    inv = 1.0 / (ROPE_THETA ** (jnp.arange(0, MOBA_HD, 2, dtype=F32) / MOBA_HD))
    ang = jnp.arange(seq, dtype=F32)[:, None] * inv[None, :]
    cos, sin = jnp.cos(ang), jnp.sin(ang)
    cos_t = jnp.tile(cos, (1, 128 // half))
    sin_t = jnp.tile(jnp.concatenate([-sin, sin], axis=1), (1, 128 // MOBA_HD))
    return cos_t, sin_t


def kernel(x, norm1_g, w_in, gla_a_up, gla_a_b, gla_norm_g, moba_qn_g, moba_kn_g, mlstm_conv_w,
           mlstm_i_b, mlstm_f_b, mlstm_norm_g, gate_b, w_br_gla, w_br_moba, w_br_mlstm, w_out,
           norm2_g, w_ff1, w_ff2):
    bsz, t, d = x.shape
    depth = w_in.shape[0]
    n = bsz * t
    assert d == D_MODEL and t % MOBA_BLOCK == 0
    tm = min(512, n)
    tb = min(512, t)

    o_ga = 4 * GLA_W
    o_moba = o_ga + GLA_LOWRANK
    o_li = o_moba + 3 * MOBA_W + 2 * MLSTM_QK + 2 * MLSTM_V
    o_gate = o_li + 2 * MLSTM_HEADS
    w_main = jnp.concatenate([w_in[:, :, :o_ga], w_in[:, :, o_moba:o_li], w_in[:, :, o_gate:]],
                             axis=2).astype(BF16)
    w_small = jnp.concatenate(
        [w_in[:, :, o_ga:o_moba], w_in[:, :, o_li:o_gate],
         jnp.zeros((depth, D_MODEL, SMALL_W - GLA_LOWRANK - 2 * MLSTM_HEADS), F32)],
        axis=2).astype(BF16)
    a_up = jnp.concatenate(
        [gla_a_up, jnp.zeros((depth, SMALL_W - GLA_LOWRANK, GLA_W), F32)], axis=1)
    ml_gate_b = jnp.concatenate(
        [jnp.zeros((depth, SM_LI), F32), mlstm_i_b, mlstm_f_b,
         jnp.zeros((depth, SMALL_W - SM_LF - MLSTM_HEADS), F32)], axis=1)
    qn_g = jnp.tile(moba_qn_g, (1, 128 // MOBA_HD))
    kn_g = jnp.tile(moba_kn_g, (1, 128 // MOBA_HD))
    w_g, w_m, w_l, w_o = (w.astype(BF16) for w in (w_br_gla, w_br_moba, w_br_mlstm, w_out))
    w1, w2 = w_ff1.astype(BF16), w_ff2.astype(BF16)

    cos_t, sin_t = _rope_tables(t)
    grp = jnp.arange(128) // MOBA_HD
    gmean = (grp[:, None] == grp[None, :]).astype(F32) / MOBA_HD
    tril = jnp.tril(jnp.ones((GLA_CHUNK, GLA_CHUNK), F32))

    xf = x.reshape(n, d)
    for l in range(depth):
        proj, small = _inproj(xf, norm1_g[l][None], w_main[l], w_small[l], tm, 1024)
        y_gla = _gla(proj, small, a_up[l], gla_a_b[l][None], gla_norm_g[l][None], tril, bsz, t, tb)
        y_ml = _mlstm(proj, small, mlstm_conv_w[l], ml_gate_b[l][None], mlstm_norm_g[l][None],
                      tril, bsz, t, tb)
        qn, kn, vt, km = _moba_prep(proj, cos_t, sin_t, qn_g[l][None], kn_g[l][None], gmean, t)
        y_moba = _moba_attn(qn, kn, vt, km, bsz, t)
        xf = _merge(xf, y_gla, y_moba, y_ml, proj, gate_b[l][None], w_g[l], w_m[l], w_l[l],
                    w_o[l], tm)
        xf = _ffn(xf, norm2_g[l][None], w1[l], w2[l], tm, 1024)
    return xf.reshape(bsz, t, d)
```

```python
import functools

import jax
import jax.numpy as jnp
from jax import lax
from jax.experimental import pallas as pl
from jax.experimental.pallas import tpu as pltpu

F32 = jnp.float32
BF16 = jnp.bfloat16
HI = lax.Precision.HIGHEST

D_MODEL = 1024
GLA_HEADS, GLA_DK, GLA_DV, GLA_LOWRANK = 4, 128, 128, 16
GLA_GATE_NORM = 16.0
GLA_CHUNK = 64
MOBA_HEADS, MOBA_HD, MOBA_BLOCK, MOBA_TOPK = 8, 64, 256, 3
ROPE_THETA = 10000.0
MLSTM_HEADS, MLSTM_DQK, MLSTM_DV = 4, 64, 128
MLSTM_CHUNK = 64
MLSTM_CONV = 4
D_FF = 4 * D_MODEL
N_BRANCH = 3
EPS = 1e-6

GLA_W = GLA_HEADS * GLA_DK
MOBA_W = MOBA_HEADS * MOBA_HD
MLSTM_QK = MLSTM_HEADS * MLSTM_DQK
MLSTM_V = MLSTM_HEADS * MLSTM_DV
MOBA_QSCALE = MOBA_HD ** -0.5 * 1.4426950408889634
MOBA_MASKED = -1e30
MOBA_VT_ROWS = MOBA_HD + 16

COL_GLA_Q, COL_GLA_K, COL_GLA_V, COL_GLA_G = 0, 1, 2, 3
COL_MOBA_Q, COL_MOBA_K, COL_MOBA_V = 4, 5, 6
COL_ML_QK, COL_ML_V, COL_ML_O = 7, 8, 9
COL_GATE0 = 5
N_MAIN = 8192
SM_GA, SM_LI, SM_LF = 0, 16, 20
SMALL_W = 128

VMEM_LIMIT = 48 * 1024 * 1024


def _dot(a, b, precision=None):
    return jnp.dot(a, b, preferred_element_type=F32, precision=precision)


def _dot_nt(a, b):
    return lax.dot_general(a, b, (((1,), (1,)), ((), ())), preferred_element_type=F32)


def _dot_tn(a, b):
    return lax.dot_general(a, b, (((0,), (0,)), ((), ())), preferred_element_type=F32)


def _sigmoid(x):
    return 1.0 / (1.0 + jnp.exp(-x))


def _log_sigmoid(x):
    return jnp.minimum(x, 0.0) - jnp.log(1.0 + jnp.exp(-jnp.abs(x)))


def _params(*sem):
    return pltpu.CompilerParams(dimension_semantics=sem, vmem_limit_bytes=VMEM_LIMIT)


def _inproj_kernel(x_ref, g_ref, w_ref, ws_ref, out_ref, small_ref, hn_ref):
    @pl.when(pl.program_id(1) == 0)
    def _():
        x = x_ref[...]
        ms = jnp.mean(x * x, axis=-1, keepdims=True)
        hn = (x * lax.rsqrt(ms + EPS) * g_ref[...]).astype(BF16)
        hn_ref[...] = hn
        small_ref[...] = _dot(hn, ws_ref[...])

    out_ref[...] = _dot(hn_ref[...], w_ref[...]).astype(out_ref.dtype)


def _inproj(x, g, w_main, w_small, tm, tn):
    n = x.shape[0]
    return pl.pallas_call(
        _inproj_kernel,
        grid=(n // tm, N_MAIN // tn),
        in_specs=[
            pl.BlockSpec((tm, D_MODEL), lambda i, j: (i, 0)),
            pl.BlockSpec((1, D_MODEL), lambda i, j: (0, 0)),
            pl.BlockSpec((D_MODEL, tn), lambda i, j: (0, j)),
            pl.BlockSpec((D_MODEL, SMALL_W), lambda i, j: (0, 0)),
        ],
        out_specs=[
            pl.BlockSpec((tm, tn), lambda i, j: (i, j)),
            pl.BlockSpec((tm, SMALL_W), lambda i, j: (i, 0)),
        ],
        out_shape=[
            jax.ShapeDtypeStruct((n, N_MAIN), BF16),
            jax.ShapeDtypeStruct((n, SMALL_W), F32),
        ],
        scratch_shapes=[pltpu.VMEM((tm, D_MODEL), BF16)],
        compiler_params=_params("parallel", "arbitrary"),
        name="inproj",
    )(x, g, w_main, w_small)


def _gla_setup(q_ref, k_ref, v_ref, g_ref, small_ref, aup_ref, ab_ref, ng_ref, tril_ref,
               o_ref, st_ref, la_ref):
    L = GLA_CHUNK

    @pl.when(pl.program_id(1) == 0)
    def _():
        st_ref[...] = jnp.zeros_like(st_ref)

    z = _dot(small_ref[...], aup_ref[...], precision=HI) + ab_ref[...]
    la_ref[...] = _log_sigmoid(z) * (1.0 / GLA_GATE_NORM)

    tril = tril_ref[...]
    causal = (lax.broadcasted_iota(jnp.int32, (L, L), 0)
              >= lax.broadcasted_iota(jnp.int32, (L, L), 1))
    scale = GLA_DK ** -0.5
    ng = ng_ref[...]

    heads = range(GLA_HEADS)
    hq = [slice(h * GLA_DK, (h + 1) * GLA_DK) for h in heads]
    hv = [slice(h * GLA_DV, (h + 1) * GLA_DV) for h in heads]

    def chunk(rows):
        bcum = _dot(tril, la_ref[rows, :], precision=HI)
        yield
        b_last = bcum[L - 1:L, :]
        q = q_ref[rows, :].astype(F32)
        k = k_ref[rows, :].astype(F32)
        v = v_ref[rows, :]
        q_dec = (q * scale * jnp.exp(bcum)).astype(BF16)
        k_inv = (k * jnp.exp(-bcum)).astype(BF16)
        k_end = (k * jnp.exp(b_last - bcum)).astype(BF16)
        decay = jnp.exp(b_last)
        st = [st_ref[h] for h in heads]
        yield
        attn = [_dot_nt(q_dec[:, hq[h]], k_inv[:, hq[h]]) for h in heads]
        o_inter = [_dot_nt(q_dec[:, hq[h]], st[h].astype(BF16)) for h in heads]
        kv = [_dot_tn(v[:, hv[h]], k_end[:, hq[h]]) for h in heads]
        yield
        attn = [jnp.where(causal, a, 0.0).astype(BF16) for a in attn]
        o = [_dot(attn[h], v[:, hv[h]]) + o_inter[h] for h in heads]
        yield
        for h in heads:
            st_ref[h] = st[h] * decay[:, hq[h]] + kv[h]
            ms = jnp.mean(o[h] * o[h], axis=-1, keepdims=True)
            y = o[h] * lax.rsqrt(ms + EPS) * ng
            gg = g_ref[rows, hv[h]].astype(F32)
            o_ref[rows, hv[h]] = (y * (gg * _sigmoid(gg))).astype(o_ref.dtype)

    return chunk


def _mlstm_setup(qk_ref, v_ref, lo_ref, small_ref, cw_ref, gb_ref, ng_ref, tril_ref,
                 out_ref, ext_ref, qkc_ref, gc_ref, st_ref, m_ref):
    tb = qk_ref.shape[0]
    L = MLSTM_CHUNK
    pad = 8

    @pl.when(pl.program_id(1) == 0)
    def _():
        st_ref[...] = jnp.zeros_like(st_ref)
        m_ref[...] = jnp.zeros_like(m_ref)
        ext_ref[0:pad, :] = jnp.zeros((pad, ext_ref.shape[1]), F32)

    x = qk_ref[...].astype(F32)
    ext_ref[pad:pad + tb, :] = x
    acc = None
    for w in range(MLSTM_CONV):
        off = pad - (MLSTM_CONV - 1) + w
        term = ext_ref[off:off + tb, :] * cw_ref[w:w + 1, :]
        acc = term if acc is None else acc + term
    qkc_ref[...] = acc
    ext_ref[0:pad, :] = x[tb - pad:tb, :]

    gsm = small_ref[...] + gb_ref[...]
    lane = lax.broadcasted_iota(jnp.int32, gsm.shape, 1)
    is_f = (lane >= SM_LF) & (lane < SM_LF + MLSTM_HEADS)
    gc_ref[...] = jnp.where(is_f, _log_sigmoid(gsm), gsm)

    tril = tril_ref[...]
    causal = (lax.broadcasted_iota(jnp.int32, (L, L), 0)
              >= lax.broadcasted_iota(jnp.int32, (L, L), 1))
    lane_c = lax.broadcasted_iota(jnp.int32, (L, SMALL_W), 1)
    is_f_c = (lane_c >= SM_LF) & (lane_c < SM_LF + MLSTM_HEADS)
    ones_aug = jnp.ones((L, MLSTM_DV), BF16)
    ng = ng_ref[...]
    kscale = MLSTM_DQK ** -0.5

    def chunk(rows):
        gates = gc_ref[rows, :]
        cum = _dot(tril, gates, precision=HI)
        yield
        pc = jnp.where(is_f_c, cum, gates)
        pr = pc.T
        heads = range(MLSTM_HEADS)
        hv = [slice(h * MLSTM_DV, (h + 1) * MLSTM_DV) for h in heads]
        qk_all = qkc_ref[rows, :]
        v_all = v_ref[rows, :].astype(BF16)
        st = [st_ref[h] for h in heads]

        dmat, inter_w, m_row, w_c, m_new, a_scale, c_scale = [], [], [], [], [], [], []
        for h in heads:
            bc = pc[:, SM_LF + h:SM_LF + h + 1]
            lic = pc[:, SM_LI + h:SM_LI + h + 1]
            br = pr[SM_LF + h:SM_LF + h + 1, :]
            lir = pr[SM_LI + h:SM_LI + h + 1, :]
            b_last = br[:, L - 1:L]
            log_d = jnp.where(causal, bc - br + lir, -jnp.inf)
            max_d = jnp.max(log_d, axis=-1, keepdims=True)
            m_loc = jnp.max(b_last - br + lir, axis=-1, keepdims=True)
            w_c.append(jnp.exp(b_last - bc + lic - m_loc))
            m_prev = m_ref[h:h + 1, 0:1]
            m_inter = bc + m_prev
            m_row.append(jnp.maximum(m_inter, max_d))
            dmat.append(jnp.exp(log_d - m_row[h]))
            inter_w.append(jnp.exp(m_inter - m_row[h]))
            m_new.append(jnp.maximum(b_last + m_prev, m_loc))
            a_scale.append(jnp.exp(b_last + m_prev - m_new[h]))
            c_scale.append(jnp.exp(m_loc - m_new[h]))

        yield
        q = [qk_all[:, h * MLSTM_DQK:(h + 1) * MLSTM_DQK].astype(BF16) for h in heads]
        kf = [qk_all[:, MLSTM_QK + h * MLSTM_DQK:MLSTM_QK + (h + 1) * MLSTM_DQK] * kscale
              for h in heads]
        v_aug = [jnp.concatenate([v_all[:, hv[h]], ones_aug], axis=1) for h in heads]
        qk = [_dot_nt(q[h], kf[h].astype(BF16)) for h in heads]
        inter = [_dot(q[h], st[h].astype(BF16)) for h in heads]
        kv = [_dot_tn((w_c[h] * kf[h]).astype(BF16), v_aug[h]) for h in heads]

        yield
        s = [(qk[h] * dmat[h]).astype(BF16) for h in heads]
        intra = [_dot(s[h], v_aug[h]) for h in heads]
        yield
        for h in heads:
            tot = intra[h] + inter_w[h] * inter[h]
            num = tot[:, :MLSTM_DV]
            den = tot[:, MLSTM_DV:]
            hid = num / jnp.maximum(jnp.abs(den), jnp.exp(-m_row[h]))
            st_ref[h] = a_scale[h] * st[h] + c_scale[h] * kv[h]
            m_ref[h:h + 1, :] = jnp.broadcast_to(m_new[h], (1, m_ref.shape[1]))
            ms = jnp.mean(hid * hid, axis=-1, keepdims=True)
            y = hid * lax.rsqrt(ms + EPS) * ng
            out_ref[rows, hv[h]] = (y * _sigmoid(lo_ref[rows, hv[h]].astype(F32))).astype(out_ref.dtype)

    return chunk


def _interleave(*gens):
    live = list(gens)
    while live:
        live = [g for g in live if next(g, True) is None]


def _recurrent_kernel(gq_ref, gk_ref, gv_ref, gg_ref, lqk_ref, lv_ref, lo_ref, small_ref,
                      aup_ref, ab_ref, gng_ref, cw_ref, gb_ref, lng_ref, tril_ref,
                      yg_ref, yl_ref,
                      gst_ref, la_ref, ext_ref, qkc_ref, gc_ref, lst_ref, m_ref):
    assert GLA_CHUNK == MLSTM_CHUNK
    L = GLA_CHUNK
    tb = gq_ref.shape[0]
    gla_chunk = _gla_setup(gq_ref, gk_ref, gv_ref, gg_ref, small_ref, aup_ref, ab_ref, gng_ref,
                           tril_ref, yg_ref, gst_ref, la_ref)
    mlstm_chunk = _mlstm_setup(lqk_ref, lv_ref, lo_ref, small_ref, cw_ref, gb_ref, lng_ref,
                               tril_ref, yl_ref, ext_ref, qkc_ref, gc_ref, lst_ref, m_ref)

    def body(c, carry):
        rows = pl.ds(pl.multiple_of(c * L, L), L)
        _interleave(gla_chunk(rows), mlstm_chunk(rows))
        return carry

    lax.fori_loop(0, tb // L, body, 0)


def _recurrent(proj, small, a_up, a_b, gla_ng, conv_w, gate_bias, ml_ng, tril, bsz, t, tb):
    n = proj.shape[0]
    nt = t // tb
    width = GLA_W
    assert MLSTM_V == width and 2 * MLSTM_QK == width

    def col(cidx):
        return pl.BlockSpec((tb, width), lambda b, s: (b * nt + s, cidx))

    const = lambda shape: pl.BlockSpec(shape, lambda b, s: (0,) * len(shape))
    out = pl.BlockSpec((tb, width), lambda b, s: (b * nt + s, 0))
    return pl.pallas_call(
        _recurrent_kernel,
        grid=(bsz, nt),
        in_specs=[
            col(COL_GLA_Q), col(COL_GLA_K), col(COL_GLA_V), col(COL_GLA_G),
            col(COL_ML_QK), col(COL_ML_V), col(COL_ML_O),
            pl.BlockSpec((tb, SMALL_W), lambda b, s: (b * nt + s, 0)),
            const((SMALL_W, GLA_W)), const((1, GLA_W)), const((1, GLA_DV)),
            const((MLSTM_CONV, 2 * MLSTM_QK)), const((1, SMALL_W)), const((1, MLSTM_DV)),
            const((GLA_CHUNK, GLA_CHUNK)),
        ],
        out_specs=[out, out],
        out_shape=[jax.ShapeDtypeStruct((n, width), BF16), jax.ShapeDtypeStruct((n, width), BF16)],
        scratch_shapes=[
            pltpu.VMEM((GLA_HEADS, GLA_DV, GLA_DK), F32),
            pltpu.VMEM((tb, GLA_W), F32),
            pltpu.VMEM((tb + 8, 2 * MLSTM_QK), F32),
            pltpu.VMEM((tb, 2 * MLSTM_QK), F32),
            pltpu.VMEM((tb, SMALL_W), F32),
            pltpu.VMEM((MLSTM_HEADS, MLSTM_DQK, 2 * MLSTM_DV), F32),
            pltpu.VMEM((8, 128), F32),
        ],
        compiler_params=_params("parallel", "arbitrary"),
        name="recurrent",
    )(proj, proj, proj, proj, proj, proj, proj, small, a_up, a_b, gla_ng, conv_w, gate_bias,
      ml_ng, tril)


def _moba_prep_kernel(q_ref, k_ref, v_ref, cos_ref, sin_ref, qg_ref, kg_ref, gm_ref,
                      qo_ref, ko_ref, vt_ref, km_ref, *, tblk):
    cos = cos_ref[...]
    sin = sin_ref[...]
    gm = gm_ref[...]
    lane = lax.broadcasted_iota(jnp.int32, cos.shape, 1)
    first_half = (lane % MOBA_HD) < (MOBA_HD // 2)
    low = lane < MOBA_HD
    j = pl.program_id(0) % tblk
    onehot = jnp.where(lane == MOBA_HD + j, 1.0, 0.0)

    def norm_rope(x, g):
        ms = _dot(x * x, gm, precision=HI)
        y = x * lax.rsqrt(ms + EPS) * g
        partner = jnp.where(first_half, pltpu.roll(y, 128 - MOBA_HD // 2, 1),
                            pltpu.roll(y, MOBA_HD // 2, 1))
        return y * cos + partner * sin

    for p in range(MOBA_W // 128):
        cs = slice(p * 128, (p + 1) * 128)
        qn = norm_rope(q_ref[:, cs].astype(F32), qg_ref[...]) * MOBA_QSCALE
        kn = norm_rope(k_ref[:, cs].astype(F32), kg_ref[...])
        v_t = v_ref[:, cs].astype(F32).T.astype(BF16)
        for h in range(2):
            hh = 2 * p + h
            qh = qn if h == 0 else pltpu.roll(qn, MOBA_HD, 1)
            kh = kn if h == 0 else pltpu.roll(kn, MOBA_HD, 1)
            qo_ref[:, hh * 128:(hh + 1) * 128] = jnp.where(low, qh, 0.0).astype(BF16)
            ka = jnp.where(low, kh, onehot)
            ko_ref[0, hh] = ka.astype(BF16)
            km_ref[0, hh:hh + 1, :] = jnp.mean(ka, axis=0, keepdims=True)
            vt_ref[0, hh, 0:MOBA_HD, :] = v_t[h * MOBA_HD:(h + 1) * MOBA_HD, :]
            vt_ref[0, hh, MOBA_HD:MOBA_VT_ROWS, :] = jnp.ones(
                (MOBA_VT_ROWS - MOBA_HD, MOBA_BLOCK), BF16)


def _moba_prep(proj, cos_t, sin_t, qg, kg, gmean, t):
    n = proj.shape[0]
    nblk = n // MOBA_BLOCK
    tblk = t // MOBA_BLOCK

    def col(cidx):
        return pl.BlockSpec((MOBA_BLOCK, MOBA_W), lambda i: (i, cidx))

    const = lambda shape: pl.BlockSpec(shape, lambda i: (0,) * len(shape))
    tab = pl.BlockSpec((MOBA_BLOCK, 128), lambda i: (i % tblk, 0))
    return pl.pallas_call(
        functools.partial(_moba_prep_kernel, tblk=tblk),
        grid=(nblk,),
        in_specs=[col(COL_MOBA_Q), col(COL_MOBA_K), col(COL_MOBA_V), tab, tab,
                  const((1, 128)), const((1, 128)), const((128, 128))],
        out_specs=[
            pl.BlockSpec((MOBA_BLOCK, MOBA_HEADS * 128), lambda i: (i, 0)),
            pl.BlockSpec((1, MOBA_HEADS, MOBA_BLOCK, 128), lambda i: (i, 0, 0, 0)),
            pl.BlockSpec((1, MOBA_HEADS, MOBA_VT_ROWS, MOBA_BLOCK), lambda i: (i, 0, 0, 0)),
            pl.BlockSpec((1, MOBA_HEADS, 128), lambda i: (i, 0, 0)),
        ],
        out_shape=[
            jax.ShapeDtypeStruct((n, MOBA_HEADS * 128), BF16),
            jax.ShapeDtypeStruct((nblk, MOBA_HEADS, MOBA_BLOCK, 128), BF16),
            jax.ShapeDtypeStruct((nblk, MOBA_HEADS, MOBA_VT_ROWS, MOBA_BLOCK), BF16),
            jax.ShapeDtypeStruct((nblk, MOBA_HEADS, 128), F32),
        ],
        compiler_params=_params("parallel"),
        name="moba_prep",
    )(proj, proj, proj, cos_t, sin_t, qg, kg, gmean)


def _moba_attn_kernel(q_ref, k_ref, vt_ref, km_ref, o_ref,
                      qa_ref, sa_ref, sb_ref, sc_ref, sd_ref, ma_ref, mb_ref, mc_ref, md_ref,
                      m_ref, acc_ref):
    i = pl.program_id(2)
    nb = k_ref.shape[0]
    B = MOBA_BLOCK
    blk = lax.broadcasted_iota(jnp.int32, (nb, B), 0)
    key_pos = lax.broadcasted_iota(jnp.int32, (B, B), 0)
    qry_pos = lax.broadcasted_iota(jnp.int32, (B, B), 1)
    neg = -jnp.inf

    for h in range(2):
        qh = q_ref[:, h * 128:(h + 1) * 128]
        g = jnp.where(blk < i, _dot_nt(km_ref[h].astype(BF16), qh), neg)
        bias_t = jnp.full((nb, B), MOBA_MASKED, F32)
        for _ in range(MOBA_TOPK):
            mx = jnp.max(g, axis=0, keepdims=True)
            first = jnp.min(jnp.where(g == mx, blk, nb), axis=0, keepdims=True)
            pick = (blk == first) & (mx > neg)
            bias_t = jnp.where(pick, 0.0, bias_t)
            g = jnp.where(pick, neg, g)
        slab_t = jnp.concatenate(
            [jnp.zeros((MOBA_HD, B), F32), bias_t, jnp.zeros((128 - MOBA_HD - nb, B), F32)], axis=0)
        qa_ref[h] = (qh.astype(F32) + slab_t.T).astype(BF16)
        m_ref[h] = jnp.full((1, B), neg, F32)
        acc_ref[h] = jnp.zeros((MOBA_VT_ROWS, B), F32)

    def stage(h, s, slot):
        s_ref, mx_ref = slot
        s_ref[h] = s
        mx_ref[h] = jnp.max(s, axis=0, keepdims=True)

    def scores_own(slot):
        for h in range(2):
            s = _dot_nt(k_ref[i, h], q_ref[:, h * 128:(h + 1) * 128])
            stage(h, jnp.where(key_pos <= qry_pos, s, neg), slot)

    def scores(j, slot):
        for h in range(2):
            stage(h, _dot_nt(k_ref[j, h], qa_ref[h]), slot)

    def accumulate(j, slot):
        s_ref, mx_ref = slot
        for h in range(2):
            m_old = m_ref[h]
            m_new = jnp.maximum(m_old, mx_ref[h])
            alpha = jnp.exp2(m_old - m_new)
            p = jnp.exp2((s_ref[h] - m_new).astype(BF16))
            m_ref[h] = m_new
            acc_ref[h] = alpha * acc_ref[h] + _dot(vt_ref[j, h], p)

    def block_at(n):
        return jnp.minimum(n - 1, i)

    slot_a, slot_b, slot_c, slot_d = ((sa_ref, ma_ref), (sb_ref, mb_ref),
                                      (sc_ref, mc_ref), (sd_ref, md_ref))
    scores_own(slot_a)
    scores(block_at(1), slot_b)

    def body(u, carry):
        n0 = 4 * u
        scores(block_at(n0 + 2), slot_c)
        accumulate(jnp.where(u == 0, i, block_at(n0)), slot_a)
        scores(block_at(n0 + 3), slot_d)
        accumulate(block_at(n0 + 1), slot_b)
        scores(block_at(n0 + 4), slot_a)
        accumulate(block_at(n0 + 2), slot_c)
        scores(block_at(n0 + 5), slot_b)
        accumulate(block_at(n0 + 3), slot_d)
        return carry

    lax.fori_loop(0, lax.shift_right_logical(i + 4, 2), body, 0)

    out_t = jnp.concatenate(
        [acc_ref[h, 0:MOBA_HD, :] / acc_ref[h, MOBA_HD:MOBA_HD + 1, :] for h in range(2)],
        axis=0)
    o_ref[...] = out_t.T.astype(o_ref.dtype)


def _moba_attn(qn, kn, vt, km, bsz, t):
    n = qn.shape[0]
    nb = t // MOBA_BLOCK
    npair = MOBA_W // 128
    return pl.pallas_call(
        _moba_attn_kernel,
        grid=(bsz, npair, nb),
        in_specs=[
            pl.BlockSpec((MOBA_BLOCK, 256), lambda b, p, i: (b * nb + i, p)),
            pl.BlockSpec((nb, 2, MOBA_BLOCK, 128), lambda b, p, i: (b, p, 0, 0)),
            pl.BlockSpec((nb, 2, MOBA_VT_ROWS, MOBA_BLOCK), lambda b, p, i: (b, p, 0, 0)),
            pl.BlockSpec((2, nb, 128), lambda b, p, i: (p, b, 0)),
        ],
        out_specs=pl.BlockSpec((MOBA_BLOCK, 128), lambda b, p, i: (b * nb + i, p)),
        out_shape=jax.ShapeDtypeStruct((n, MOBA_W), BF16),
        scratch_shapes=[
            pltpu.VMEM((2, MOBA_BLOCK, 128), BF16),
            pltpu.VMEM((2, MOBA_BLOCK, MOBA_BLOCK), F32),
            pltpu.VMEM((2, MOBA_BLOCK, MOBA_BLOCK), F32),
            pltpu.VMEM((2, MOBA_BLOCK, MOBA_BLOCK), F32),
            pltpu.VMEM((2, MOBA_BLOCK, MOBA_BLOCK), F32),
            pltpu.VMEM((2, 1, MOBA_BLOCK), F32),
            pltpu.VMEM((2, 1, MOBA_BLOCK), F32),
            pltpu.VMEM((2, 1, MOBA_BLOCK), F32),
            pltpu.VMEM((2, 1, MOBA_BLOCK), F32),
            pltpu.VMEM((2, 1, MOBA_BLOCK), F32),
            pltpu.VMEM((2, MOBA_VT_ROWS, MOBA_BLOCK), F32),
        ],
        compiler_params=_params("parallel", "parallel", "arbitrary"),
        name="moba_attn",
    )(qn, kn, vt, jnp.transpose(km, (1, 0, 2)))


def _merge_kernel(x_ref, yg_ref, ym_ref, yl_ref, g0_ref, g1_ref, g2_ref, gb_ref,
                  wg_ref, wm_ref, wl_ref, wo_ref, o_ref):
    gb = gb_ref[...]
    def gate(g_ref, c):
        return _sigmoid(g_ref[...].astype(F32) + gb[:, c * D_MODEL:(c + 1) * D_MODEL])

    mixed = (gate(g0_ref, 0) * _dot(yg_ref[...], wg_ref[...])
             + gate(g1_ref, 1) * _dot(ym_ref[...], wm_ref[...])
             + gate(g2_ref, 2) * _dot(yl_ref[...], wl_ref[...]))
    o_ref[...] = x_ref[...] + _dot(mixed.astype(BF16), wo_ref[...])


def _merge(x, y_gla, y_moba, y_ml, proj, gate_b, w_g, w_m, w_l, w_o, tm):
    n = x.shape[0]
    row = lambda w: pl.BlockSpec((tm, w), lambda i: (i, 0))
    const = lambda shape: pl.BlockSpec(shape, lambda i: (0,) * len(shape))
    gate = lambda c: pl.BlockSpec((tm, D_MODEL), lambda i: (i, COL_GATE0 + c))
    return pl.pallas_call(
        _merge_kernel,
        grid=(n // tm,),
        in_specs=[row(D_MODEL), row(GLA_W), row(MOBA_W), row(MLSTM_V),
                  gate(0), gate(1), gate(2), const((1, N_BRANCH * D_MODEL)),
                  const((GLA_W, D_MODEL)), const((MOBA_W, D_MODEL)), const((MLSTM_V, D_MODEL)),
                  const((D_MODEL, D_MODEL))],
        out_specs=row(D_MODEL),
        out_shape=jax.ShapeDtypeStruct((n, D_MODEL), F32),
        compiler_params=_params("parallel"),
        name="merge",
    )(x, y_gla, y_moba, y_ml, proj, proj, proj, gate_b, w_g, w_m, w_l, w_o)


def _ffn_kernel(x_ref, g_ref, w1_ref, w2_ref, o_ref, h_ref):
    @pl.when(pl.program_id(1) == 0)
    def _():
        x = x_ref[...]
        ms = jnp.mean(x * x, axis=-1, keepdims=True)
        h_ref[...] = (x * lax.rsqrt(ms + EPS) * g_ref[...]).astype(BF16)
        o_ref[...] = x

    a = jnp.square(jnp.maximum(_dot(h_ref[...], w1_ref[...]), 0.0)).astype(BF16)
    o_ref[...] += _dot(a, w2_ref[...])


def _ffn(x, g, w1, w2, tm, tf):
    n = x.shape[0]
    return pl.pallas_call(
        _ffn_kernel,
        grid=(n // tm, D_FF // tf),
        in_specs=[
            pl.BlockSpec((tm, D_MODEL), lambda i, j: (i, 0)),
            pl.BlockSpec((1, D_MODEL), lambda i, j: (0, 0)),
            pl.BlockSpec((D_MODEL, tf), lambda i, j: (0, j)),
            pl.BlockSpec((tf, D_MODEL), lambda i, j: (j, 0)),
        ],
        out_specs=pl.BlockSpec((tm, D_MODEL), lambda i, j: (i, 0)),
        out_shape=jax.ShapeDtypeStruct((n, D_MODEL), F32),
        scratch_shapes=[pltpu.VMEM((tm, D_MODEL), BF16)],
        compiler_params=_params("parallel", "arbitrary"),
        name="ffn",
    )(x, g, w1, w2)


def _rope_tables(seq):
    half = MOBA_HD // 2
    inv = 1.0 / (ROPE_THETA ** (jnp.arange(0, MOBA_HD, 2, dtype=F32) / MOBA_HD))
    ang = jnp.arange(seq, dtype=F32)[:, None] * inv[None, :]
    cos, sin = jnp.cos(ang), jnp.sin(ang)
    cos_t = jnp.tile(cos, (1, 128 // half))
    sin_t = jnp.tile(jnp.concatenate([-sin, sin], axis=1), (1, 128 // MOBA_HD))
    return cos_t, sin_t


def kernel(x, norm1_g, w_in, gla_a_up, gla_a_b, gla_norm_g, moba_qn_g, moba_kn_g, mlstm_conv_w,
           mlstm_i_b, mlstm_f_b, mlstm_norm_g, gate_b, w_br_gla, w_br_moba, w_br_mlstm, w_out,
           norm2_g, w_ff1, w_ff2):
    bsz, t, d = x.shape
    depth = w_in.shape[0]
    n = bsz * t
    assert d == D_MODEL and t % MOBA_BLOCK == 0
    tm = min(512, n)
    tm_big = min(1024, n)
    tb = min(512, t)

    o_ga = 4 * GLA_W
    o_moba = o_ga + GLA_LOWRANK
    o_li = o_moba + 3 * MOBA_W + 2 * MLSTM_QK + 2 * MLSTM_V
    o_gate = o_li + 2 * MLSTM_HEADS
    w_main = jnp.concatenate([w_in[:, :, :o_ga], w_in[:, :, o_moba:o_li], w_in[:, :, o_gate:]],
                             axis=2).astype(BF16)
    w_small = jnp.concatenate(
        [w_in[:, :, o_ga:o_moba], w_in[:, :, o_li:o_gate],
         jnp.zeros((depth, D_MODEL, SMALL_W - GLA_LOWRANK - 2 * MLSTM_HEADS), F32)],
        axis=2).astype(BF16)
    a_up = jnp.concatenate(
        [gla_a_up, jnp.zeros((depth, SMALL_W - GLA_LOWRANK, GLA_W), F32)], axis=1)
    ml_gate_b = jnp.concatenate(
        [jnp.zeros((depth, SM_LI), F32), mlstm_i_b, mlstm_f_b,
         jnp.zeros((depth, SMALL_W - SM_LF - MLSTM_HEADS), F32)], axis=1)
    qn_g = jnp.tile(moba_qn_g, (1, 128 // MOBA_HD))
    kn_g = jnp.tile(moba_kn_g, (1, 128 // MOBA_HD))
    w_g, w_m, w_l, w_o = (w.astype(BF16) for w in (w_br_gla, w_br_moba, w_br_mlstm, w_out))
    w1, w2 = w_ff1.astype(BF16), w_ff2.astype(BF16)

    cos_t, sin_t = _rope_tables(t)
    grp = jnp.arange(128) // MOBA_HD
    gmean = (grp[:, None] == grp[None, :]).astype(F32) / MOBA_HD
    tril = jnp.tril(jnp.ones((GLA_CHUNK, GLA_CHUNK), F32))

    xf = x.reshape(n, d)
    for l in range(depth):
        proj, small = _inproj(xf, norm1_g[l][None], w_main[l], w_small[l], tm_big, 1024)
        y_gla, y_ml = _recurrent(proj, small, a_up[l], gla_a_b[l][None], gla_norm_g[l][None],
                                 mlstm_conv_w[l], ml_gate_b[l][None], mlstm_norm_g[l][None],
                                 tril, bsz, t, tb)
        qn, kn, vt, km = _moba_prep(proj, cos_t, sin_t, qn_g[l][None], kn_g[l][None], gmean, t)
        y_moba = _moba_attn(qn, kn, vt, km, bsz, t)
        xf = _merge(xf, y_gla, y_moba, y_ml, proj, gate_b[l][None], w_g[l], w_m[l], w_l[l],
                    w_o[l], tm)
        xf = _ffn(xf, norm2_g[l][None], w1[l], w2[l], tm_big, 1024)
    return xf.reshape(bsz, t, d)
```

```python
import functools

import jax
import jax.numpy as jnp
from jax import lax
from jax.experimental import pallas as pl
from jax.experimental.pallas import tpu as pltpu

F32 = jnp.float32
BF16 = jnp.bfloat16
HI = lax.Precision.HIGHEST

D_MODEL = 1024
GLA_HEADS, GLA_DK, GLA_DV, GLA_LOWRANK = 4, 128, 128, 16
GLA_GATE_NORM = 16.0
GLA_CHUNK = 256
GLA_SUB = 64
MOBA_HEADS, MOBA_HD, MOBA_BLOCK, MOBA_TOPK = 8, 64, 256, 3
ROPE_THETA = 10000.0
MLSTM_HEADS, MLSTM_DQK, MLSTM_DV = 4, 64, 128
MLSTM_CHUNK = 256
MLSTM_CONV = 4
D_FF = 4 * D_MODEL
N_BRANCH = 3
EPS = 1e-6

GLA_W = GLA_HEADS * GLA_DK
MOBA_W = MOBA_HEADS * MOBA_HD
MLSTM_QK = MLSTM_HEADS * MLSTM_DQK
MLSTM_V = MLSTM_HEADS * MLSTM_DV
MOBA_QSCALE = MOBA_HD ** -0.5 * 1.4426950408889634
MOBA_MASKED = -1e30
MOBA_VT_ROWS = MOBA_HD + 16

COL_GLA_Q, COL_GLA_K, COL_GLA_V, COL_GLA_G = 0, 1, 2, 3
COL_MOBA_Q, COL_MOBA_K, COL_MOBA_V = 4, 5, 6
COL_ML_QK, COL_ML_V, COL_ML_O = 7, 8, 9
COL_GATE0 = 5
N_MAIN = 8192
SM_GA, SM_LI, SM_LF = 0, 16, 20
SMALL_W = 128

VMEM_LIMIT = 48 * 1024 * 1024


def _dot(a, b, precision=None):
    return jnp.dot(a, b, preferred_element_type=F32, precision=precision)


def _dot_nt(a, b):
    return lax.dot_general(a, b, (((1,), (1,)), ((), ())), preferred_element_type=F32)


def _dot_tn(a, b):
    return lax.dot_general(a, b, (((0,), (0,)), ((), ())), preferred_element_type=F32)


def _cumsum_rows(tril_bf16, x):
    hi = x.astype(BF16)
    rest = x - hi.astype(F32)
    mid = rest.astype(BF16)
    lo = (rest - mid.astype(F32)).astype(BF16)
    return _dot(tril_bf16, hi) + _dot(tril_bf16, mid) + _dot(tril_bf16, lo)


def _sigmoid(x):
    return 1.0 / (1.0 + jnp.exp(-x))


def _log_sigmoid(x):
    return jnp.minimum(x, 0.0) - jnp.log(1.0 + jnp.exp(-jnp.abs(x)))


def _params(*sem):
    return pltpu.CompilerParams(dimension_semantics=sem, vmem_limit_bytes=VMEM_LIMIT)


def _inproj_kernel(x_ref, g_ref, w_ref, ws_ref, out_ref, small_ref, hn_ref):
    @pl.when(pl.program_id(1) == 0)
    def _():
        x = x_ref[...]
        ms = jnp.mean(x * x, axis=-1, keepdims=True)
        hn = (x * lax.rsqrt(ms + EPS) * g_ref[...]).astype(BF16)
        hn_ref[...] = hn
        small_ref[...] = _dot(hn, ws_ref[...])

    out_ref[...] = _dot(hn_ref[...], w_ref[...]).astype(out_ref.dtype)


def _inproj(x, g, w_main, w_small, tm, tn):
    n = x.shape[0]
    return pl.pallas_call(
        _inproj_kernel,
        grid=(n // tm, N_MAIN // tn),
        in_specs=[
            pl.BlockSpec((tm, D_MODEL), lambda i, j: (i, 0)),
            pl.BlockSpec((1, D_MODEL), lambda i, j: (0, 0)),
            pl.BlockSpec((D_MODEL, tn), lambda i, j: (0, j)),
            pl.BlockSpec((D_MODEL, SMALL_W), lambda i, j: (0, 0)),
        ],
        out_specs=[
            pl.BlockSpec((tm, tn), lambda i, j: (i, j)),
            pl.BlockSpec((tm, SMALL_W), lambda i, j: (i, 0)),
        ],
        out_shape=[
            jax.ShapeDtypeStruct((n, N_MAIN), BF16),
            jax.ShapeDtypeStruct((n, SMALL_W), F32),
        ],
        scratch_shapes=[pltpu.VMEM((tm, D_MODEL), BF16)],
        compiler_params=_params("parallel", "arbitrary"),
        name="inproj",
    )(x, g, w_main, w_small)


def _gla_setup(q_ref, k_ref, v_ref, g_ref, small_ref, aup_ref, ab_ref, ng_ref, tril_ref,
               o_ref, st_ref, la_ref):
    L = GLA_CHUNK

    @pl.when(pl.program_id(1) == 0)
    def _():
        st_ref[...] = jnp.zeros_like(st_ref)

    z = _dot(small_ref[...], aup_ref[...], precision=HI) + ab_ref[...]
    la_ref[...] = _log_sigmoid(z) * (1.0 / GLA_GATE_NORM)

    tril = tril_ref[...]
    S = GLA_SUB
    nsub = L // S
    scale = GLA_DK ** -0.5
    ng = ng_ref[...]

    heads = range(GLA_HEADS)
    hq = [slice(h * GLA_DK, (h + 1) * GLA_DK) for h in heads]
    hv = [slice(h * GLA_DV, (h + 1) * GLA_DV) for h in heads]
    sub = [slice(a * S, (a + 1) * S) for a in range(nsub)]

    def chunk(rows):
        bcum = _cumsum_rows(tril, la_ref[rows, :])
        yield
        b_last = bcum[L - 1:L, :]
        q = q_ref[rows, :].astype(F32) * scale
        k = k_ref[rows, :].astype(F32)
        v = v_ref[rows, :]
        c = [jnp.zeros((1, GLA_W), F32)] + [bcum[a * S - 1:a * S, :] for a in range(1, nsub)]
        c_rows = jnp.concatenate([jnp.broadcast_to(c[a], (S, GLA_W)) for a in range(nsub)], axis=0)
        q_loc = q * jnp.exp(bcum - c_rows)
        k_loc = (k * jnp.exp(c_rows - bcum)).astype(BF16)
        q_dec = (q * jnp.exp(bcum)).astype(BF16)
        k_end = (k * jnp.exp(b_last - bcum)).astype(BF16)
        decay = jnp.exp(b_last)
        q_vs = [jnp.concatenate(
            [(q_loc[sub[a], :] if a == b else q_loc[sub[a], :] * jnp.exp(c[a] - c[b])).astype(BF16)
             for a in range(b, nsub)], axis=0) for b in range(nsub)]
        st = [st_ref[h] for h in heads]
        yield
        attn = [[_dot_nt(q_vs[b][:, hq[h]], k_loc[sub[b], hq[h]]) for h in heads]
                for b in range(nsub)]
        o_inter = [_dot_nt(q_dec[:, hq[h]], st[h].astype(BF16)) for h in heads]
        kv = [_dot_tn(v[:, hv[h]], k_end[:, hq[h]]) for h in heads]
        yield
        pv = []
        for b in range(nsub):
            shape = (L - b * S, S)
            keep = (lax.broadcasted_iota(jnp.int32, shape, 0)
                    >= lax.broadcasted_iota(jnp.int32, shape, 1))
            pv.append([_dot(jnp.where(keep, attn[b][h], 0.0).astype(BF16), v[sub[b], hv[h]])
                       for h in heads])
        yield
        for h in heads:
            o = jnp.concatenate(
                [o_inter[h][sub[a], :] + sum(pv[b][h][(a - b) * S:(a - b + 1) * S, :]
                                             for b in range(a + 1)) for a in range(nsub)], axis=0)
            st_ref[h] = st[h] * decay[:, hq[h]] + kv[h]
            ms = jnp.mean(o * o, axis=-1, keepdims=True)
            y = o * lax.rsqrt(ms + EPS) * ng
            gg = g_ref[rows, hv[h]].astype(F32)
            o_ref[rows, hv[h]] = (y * (gg * _sigmoid(gg))).astype(o_ref.dtype)

    return chunk


def _mlstm_setup(qk_ref, v_ref, lo_ref, small_ref, cw_ref, gb_ref, ng_ref, tril_ref,
                 out_ref, ext_ref, qkc_ref, gc_ref, st_ref, m_ref):
    tb = qk_ref.shape[0]
    L = MLSTM_CHUNK
    pad = 8

    @pl.when(pl.program_id(1) == 0)
    def _():
        st_ref[...] = jnp.zeros_like(st_ref)
        m_ref[...] = jnp.zeros_like(m_ref)
        ext_ref[0:pad, :] = jnp.zeros((pad, ext_ref.shape[1]), F32)

    x = qk_ref[...].astype(F32)
    ext_ref[pad:pad + tb, :] = x
    acc = None
    for w in range(MLSTM_CONV):
        off = pad - (MLSTM_CONV - 1) + w
        term = ext_ref[off:off + tb, :] * cw_ref[w:w + 1, :]
        acc = term if acc is None else acc + term
    qkc_ref[...] = acc
    ext_ref[0:pad, :] = x[tb - pad:tb, :]

    gsm = small_ref[...] + gb_ref[...]
    lane = lax.broadcasted_iota(jnp.int32, gsm.shape, 1)
    is_f = (lane >= SM_LF) & (lane < SM_LF + MLSTM_HEADS)
    gc_ref[...] = jnp.where(is_f, _log_sigmoid(gsm), gsm)

    tril = tril_ref[...]
    causal = (lax.broadcasted_iota(jnp.int32, (L, L), 0)
              >= lax.broadcasted_iota(jnp.int32, (L, L), 1))
    lane_c = lax.broadcasted_iota(jnp.int32, (L, SMALL_W), 1)
    is_f_c = (lane_c >= SM_LF) & (lane_c < SM_LF + MLSTM_HEADS)
    ones_aug = jnp.ones((L, MLSTM_DV), BF16)
    ng = ng_ref[...]
    kscale = MLSTM_DQK ** -0.5

    def chunk(rows):
        gates = gc_ref[rows, :]
        cum = _cumsum_rows(tril, gates)
        yield
        pc = jnp.where(is_f_c, cum, gates)
        pr = pc.T
        heads = range(MLSTM_HEADS)
        hv = [slice(h * MLSTM_DV, (h + 1) * MLSTM_DV) for h in heads]
        qk_all = qkc_ref[rows, :]
        v_all = v_ref[rows, :].astype(BF16)
        st = [st_ref[h] for h in heads]

        dmat, inter_w, m_row, w_c, m_new, a_scale, c_scale = [], [], [], [], [], [], []
        for h in heads:
            bc = pc[:, SM_LF + h:SM_LF + h + 1]
            lic = pc[:, SM_LI + h:SM_LI + h + 1]
            br = pr[SM_LF + h:SM_LF + h + 1, :]
            lir = pr[SM_LI + h:SM_LI + h + 1, :]
            b_last = br[:, L - 1:L]
            log_d = jnp.where(causal, bc - br + lir, -jnp.inf)
            max_d = jnp.max(log_d, axis=-1, keepdims=True)
            m_loc = jnp.max(b_last - br + lir, axis=-1, keepdims=True)
            w_c.append(jnp.exp(b_last - bc + lic - m_loc))
            m_prev = m_ref[h:h + 1, 0:1]
            m_inter = bc + m_prev
            m_row.append(jnp.maximum(m_inter, max_d))
            dmat.append(jnp.exp(log_d - m_row[h]))
            inter_w.append(jnp.exp(m_inter - m_row[h]))
            m_new.append(jnp.maximum(b_last + m_prev, m_loc))
            a_scale.append(jnp.exp(b_last + m_prev - m_new[h]))
            c_scale.append(jnp.exp(m_loc - m_new[h]))

        yield
        q = [qk_all[:, h * MLSTM_DQK:(h + 1) * MLSTM_DQK].astype(BF16) for h in heads]
        kf = [qk_all[:, MLSTM_QK + h * MLSTM_DQK:MLSTM_QK + (h + 1) * MLSTM_DQK] * kscale
              for h in heads]
        v_aug = [jnp.concatenate([v_all[:, hv[h]], ones_aug], axis=1) for h in heads]
        qk = [_dot_nt(q[h], kf[h].astype(BF16)) for h in heads]
        inter = [_dot(q[h], st[h].astype(BF16)) for h in heads]
        kv = [_dot_tn((w_c[h] * kf[h]).astype(BF16), v_aug[h]) for h in heads]

        yield
        s = [(qk[h] * dmat[h]).astype(BF16) for h in heads]
        intra = [_dot(s[h], v_aug[h]) for h in heads]
        yield
        for h in heads:
            tot = intra[h] + inter_w[h] * inter[h]
            num = tot[:, :MLSTM_DV]
            den = tot[:, MLSTM_DV:]
            hid = num / jnp.maximum(jnp.abs(den), jnp.exp(-m_row[h]))
            st_ref[h] = a_scale[h] * st[h] + c_scale[h] * kv[h]
            m_ref[h:h + 1, :] = jnp.broadcast_to(m_new[h], (1, m_ref.shape[1]))
            ms = jnp.mean(hid * hid, axis=-1, keepdims=True)
            y = hid * lax.rsqrt(ms + EPS) * ng
            out_ref[rows, hv[h]] = (y * _sigmoid(lo_ref[rows, hv[h]].astype(F32))).astype(out_ref.dtype)

    return chunk


def _interleave(*gens):
    live = list(gens)
    while live:
        live = [g for g in live if next(g, True) is None]


def _recurrent_kernel(gq_ref, gk_ref, gv_ref, gg_ref, lqk_ref, lv_ref, lo_ref, small_ref,
                      aup_ref, ab_ref, gng_ref, cw_ref, gb_ref, lng_ref, tril_ref,
                      yg_ref, yl_ref,
                      gst_ref, la_ref, ext_ref, qkc_ref, gc_ref, lst_ref, m_ref):
    assert GLA_CHUNK == MLSTM_CHUNK
    L = GLA_CHUNK
    tb = gq_ref.shape[0]
    gla_chunk = _gla_setup(gq_ref, gk_ref, gv_ref, gg_ref, small_ref, aup_ref, ab_ref, gng_ref,
                           tril_ref, yg_ref, gst_ref, la_ref)
    mlstm_chunk = _mlstm_setup(lqk_ref, lv_ref, lo_ref, small_ref, cw_ref, gb_ref, lng_ref,
                               tril_ref, yl_ref, ext_ref, qkc_ref, gc_ref, lst_ref, m_ref)

    def body(c, carry):
        rows = pl.ds(pl.multiple_of(c * L, L), L)
        _interleave(gla_chunk(rows), mlstm_chunk(rows))
        return carry

    lax.fori_loop(0, tb // L, body, 0)


def _recurrent(proj, small, a_up, a_b, gla_ng, conv_w, gate_bias, ml_ng, tril, bsz, t, tb):
    n = proj.shape[0]
    nt = t // tb
    width = GLA_W
    assert MLSTM_V == width and 2 * MLSTM_QK == width

    def col(cidx):
        return pl.BlockSpec((tb, width), lambda b, s: (b * nt + s, cidx))

    const = lambda shape: pl.BlockSpec(shape, lambda b, s: (0,) * len(shape))
    out = pl.BlockSpec((tb, width), lambda b, s: (b * nt + s, 0))
    return pl.pallas_call(
        _recurrent_kernel,
        grid=(bsz, nt),
        in_specs=[
            col(COL_GLA_Q), col(COL_GLA_K), col(COL_GLA_V), col(COL_GLA_G),
            col(COL_ML_QK), col(COL_ML_V), col(COL_ML_O),
            pl.BlockSpec((tb, SMALL_W), lambda b, s: (b * nt + s, 0)),
            const((SMALL_W, GLA_W)), const((1, GLA_W)), const((1, GLA_DV)),
            const((MLSTM_CONV, 2 * MLSTM_QK)), const((1, SMALL_W)), const((1, MLSTM_DV)),
            const((GLA_CHUNK, GLA_CHUNK)),
        ],
        out_specs=[out, out],
        out_shape=[jax.ShapeDtypeStruct((n, width), BF16), jax.ShapeDtypeStruct((n, width), BF16)],
        scratch_shapes=[
            pltpu.VMEM((GLA_HEADS, GLA_DV, GLA_DK), F32),
            pltpu.VMEM((tb, GLA_W), F32),
            pltpu.VMEM((tb + 8, 2 * MLSTM_QK), F32),
            pltpu.VMEM((tb, 2 * MLSTM_QK), F32),
            pltpu.VMEM((tb, SMALL_W), F32),
            pltpu.VMEM((MLSTM_HEADS, MLSTM_DQK, 2 * MLSTM_DV), F32),
            pltpu.VMEM((8, 128), F32),
        ],
        compiler_params=_params("parallel", "arbitrary"),
        name="recurrent",
    )(proj, proj, proj, proj, proj, proj, proj, small, a_up, a_b, gla_ng, conv_w, gate_bias,
      ml_ng, tril)


def _moba_prep_kernel(q_ref, k_ref, v_ref, cos_ref, sin_ref, qg_ref, kg_ref, gm_ref,
                      qo_ref, ko_ref, vt_ref, km_ref, *, tblk):
    cos = cos_ref[...]
    sin = sin_ref[...]
    gm = gm_ref[...]
    lane = lax.broadcasted_iota(jnp.int32, cos.shape, 1)
    first_half = (lane % MOBA_HD) < (MOBA_HD // 2)
    low = lane < MOBA_HD
    j = pl.program_id(0) % tblk
    onehot = jnp.where(lane == MOBA_HD + j, 1.0, 0.0)

    def norm_rope(x, g):
        ms = _dot(x * x, gm, precision=HI)
        y = x * lax.rsqrt(ms + EPS) * g
        partner = jnp.where(first_half, pltpu.roll(y, 128 - MOBA_HD // 2, 1),
                            pltpu.roll(y, MOBA_HD // 2, 1))
        return y * cos + partner * sin

    for p in range(MOBA_W // 128):
        cs = slice(p * 128, (p + 1) * 128)
        qn = norm_rope(q_ref[:, cs].astype(F32), qg_ref[...]) * MOBA_QSCALE
        kn = norm_rope(k_ref[:, cs].astype(F32), kg_ref[...])
        v_t = v_ref[:, cs].astype(F32).T.astype(BF16)
        for h in range(2):
            hh = 2 * p + h
            qh = qn if h == 0 else pltpu.roll(qn, MOBA_HD, 1)
            kh = kn if h == 0 else pltpu.roll(kn, MOBA_HD, 1)
            qo_ref[:, hh * 128:(hh + 1) * 128] = jnp.where(low, qh, 0.0).astype(BF16)
            ka = jnp.where(low, kh, onehot)
            ko_ref[0, hh] = ka.astype(BF16)
            km_ref[0, hh:hh + 1, :] = jnp.mean(ka, axis=0, keepdims=True)
            vt_ref[0, hh, 0:MOBA_HD, :] = v_t[h * MOBA_HD:(h + 1) * MOBA_HD, :]
            vt_ref[0, hh, MOBA_HD:MOBA_VT_ROWS, :] = jnp.ones(
                (MOBA_VT_ROWS - MOBA_HD, MOBA_BLOCK), BF16)


def _moba_prep(proj, cos_t, sin_t, qg, kg, gmean, t):
    n = proj.shape[0]
    nblk = n // MOBA_BLOCK
    tblk = t // MOBA_BLOCK

    def col(cidx):
        return pl.BlockSpec((MOBA_BLOCK, MOBA_W), lambda i: (i, cidx))

    const = lambda shape: pl.BlockSpec(shape, lambda i: (0,) * len(shape))
    tab = pl.BlockSpec((MOBA_BLOCK, 128), lambda i: (i % tblk, 0))
    return pl.pallas_call(
        functools.partial(_moba_prep_kernel, tblk=tblk),
        grid=(nblk,),
        in_specs=[col(COL_MOBA_Q), col(COL_MOBA_K), col(COL_MOBA_V), tab, tab,
                  const((1, 128)), const((1, 128)), const((128, 128))],
        out_specs=[
            pl.BlockSpec((MOBA_BLOCK, MOBA_HEADS * 128), lambda i: (i, 0)),
            pl.BlockSpec((1, MOBA_HEADS, MOBA_BLOCK, 128), lambda i: (i, 0, 0, 0)),
            pl.BlockSpec((1, MOBA_HEADS, MOBA_VT_ROWS, MOBA_BLOCK), lambda i: (i, 0, 0, 0)),
            pl.BlockSpec((1, MOBA_HEADS, 128), lambda i: (i, 0, 0)),
        ],
        out_shape=[
            jax.ShapeDtypeStruct((n, MOBA_HEADS * 128), BF16),
            jax.ShapeDtypeStruct((nblk, MOBA_HEADS, MOBA_BLOCK, 128), BF16),
            jax.ShapeDtypeStruct((nblk, MOBA_HEADS, MOBA_VT_ROWS, MOBA_BLOCK), BF16),
            jax.ShapeDtypeStruct((nblk, MOBA_HEADS, 128), F32),
        ],
        compiler_params=_params("parallel"),
        name="moba_prep",
    )(proj, proj, proj, cos_t, sin_t, qg, kg, gmean)


def _moba_attn_kernel(q_ref, k_ref, vt_ref, km_ref, o_ref,
                      qa_ref, sa_ref, sb_ref, sc_ref, sd_ref, ma_ref, mb_ref, mc_ref, md_ref,
                      m_ref, acc_ref):
    i = pl.program_id(1)
    nbat, nb = k_ref.shape[0], k_ref.shape[1]
    B = MOBA_BLOCK
    blk = lax.broadcasted_iota(jnp.int32, (nb, B), 0)
    key_pos = lax.broadcasted_iota(jnp.int32, (B, B), 0)
    qry_pos = lax.broadcasted_iota(jnp.int32, (B, B), 1)
    neg = -jnp.inf
    streams = [(bb, h) for bb in range(nbat) for h in range(2)]

    def q_slab(bb, h):
        return q_ref[bb, :, h * 128:(h + 1) * 128]

    gates = [_dot_nt(km_ref[h, bb].astype(BF16), q_slab(bb, h)) for bb, h in streams]
    for st, (bb, h) in enumerate(streams):
        g = jnp.where(blk < i, gates[st], neg)
        bias_t = jnp.full((nb, B), MOBA_MASKED, F32)
        for _ in range(MOBA_TOPK):
            mx = jnp.max(g, axis=0, keepdims=True)
            first = jnp.min(jnp.where(g == mx, blk, nb), axis=0, keepdims=True)
            pick = (blk == first) & (mx > neg)
            bias_t = jnp.where(pick, 0.0, bias_t)
            g = jnp.where(pick, neg, g)
        slab_t = jnp.concatenate(
            [jnp.zeros((MOBA_HD, B), F32), bias_t, jnp.zeros((128 - MOBA_HD - nb, B), F32)], axis=0)
        qa_ref[st] = (q_slab(bb, h).astype(F32) + slab_t.T).astype(BF16)
        m_ref[st] = jnp.full((1, B), neg, F32)
        acc_ref[st] = jnp.zeros((MOBA_VT_ROWS, B), F32)

    def stage(st, s, slot):
        s_ref, mx_ref = slot
        s_ref[st] = s
        mx_ref[st] = jnp.max(s, axis=0, keepdims=True)

    def scores_own(slot):
        raw = [_dot_nt(k_ref[bb, i, h], q_slab(bb, h)) for bb, h in streams]
        for st in range(len(streams)):
            stage(st, jnp.where(key_pos <= qry_pos, raw[st], neg), slot)

    def scores(j, slot):
        raw = [_dot_nt(k_ref[bb, j, h], qa_ref[st]) for st, (bb, h) in enumerate(streams)]
        for st in range(len(streams)):
            stage(st, raw[st], slot)

    def accumulate(j, slot):
        s_ref, mx_ref = slot
        for st, (bb, h) in enumerate(streams):
            m_old = m_ref[st]
            m_new = jnp.maximum(m_old, mx_ref[st])
            alpha = jnp.exp2(m_old - m_new)
            p = jnp.exp2((s_ref[st] - m_new).astype(BF16))
            m_ref[st] = m_new
            acc_ref[st] = alpha * acc_ref[st] + _dot(vt_ref[bb, j, h], p)

    def block_at(n):
        return jnp.minimum(n - 1, i)

    slot_a, slot_b, slot_c, slot_d = ((sa_ref, ma_ref), (sb_ref, mb_ref),
                                      (sc_ref, mc_ref), (sd_ref, md_ref))
    scores_own(slot_a)
    scores(block_at(1), slot_b)

    def body(u, carry):
        n0 = 4 * u
        scores(block_at(n0 + 2), slot_c)
        accumulate(jnp.where(u == 0, i, block_at(n0)), slot_a)
        scores(block_at(n0 + 3), slot_d)
        accumulate(block_at(n0 + 1), slot_b)
        scores(block_at(n0 + 4), slot_a)
        accumulate(block_at(n0 + 2), slot_c)
        scores(block_at(n0 + 5), slot_b)
        accumulate(block_at(n0 + 3), slot_d)
        return carry

    lax.fori_loop(0, lax.shift_right_logical(i + 4, 2), body, 0)

    for bb in range(nbat):
        out_t = jnp.concatenate(
            [acc_ref[2 * bb + h, 0:MOBA_HD, :] / acc_ref[2 * bb + h, MOBA_HD:MOBA_HD + 1, :]
             for h in range(2)], axis=0)
        o_ref[bb] = out_t.T.astype(o_ref.dtype)


def _moba_attn(qn, kn, vt, km, bsz, t):
    nb = t // MOBA_BLOCK
    npair = MOBA_W // 128
    ns = 2 * bsz
    scores_slot = pltpu.VMEM((ns, MOBA_BLOCK, MOBA_BLOCK), F32)
    row_slot = pltpu.VMEM((ns, 1, MOBA_BLOCK), F32)
    out = pl.pallas_call(
        _moba_attn_kernel,
        grid=(npair, nb),
        in_specs=[
            pl.BlockSpec((bsz, MOBA_BLOCK, 256), lambda p, i: (0, i, p)),
            pl.BlockSpec((bsz, nb, 2, MOBA_BLOCK, 128), lambda p, i: (0, 0, p, 0, 0)),
            pl.BlockSpec((bsz, nb, 2, MOBA_VT_ROWS, MOBA_BLOCK), lambda p, i: (0, 0, p, 0, 0)),
            pl.BlockSpec((2, bsz, nb, 128), lambda p, i: (p, 0, 0, 0)),
        ],
        out_specs=pl.BlockSpec((bsz, MOBA_BLOCK, 128), lambda p, i: (0, i, p)),
        out_shape=jax.ShapeDtypeStruct((bsz, t, MOBA_W), BF16),
        scratch_shapes=[
            pltpu.VMEM((ns, MOBA_BLOCK, 128), BF16),
            scores_slot, scores_slot, scores_slot, scores_slot,
            row_slot, row_slot, row_slot, row_slot,
            row_slot,
            pltpu.VMEM((ns, MOBA_VT_ROWS, MOBA_BLOCK), F32),
        ],
        compiler_params=_params("parallel", "arbitrary"),
        name="moba_attn",
    )(qn.reshape(bsz, t, MOBA_HEADS * 128),
      kn.reshape(bsz, nb, MOBA_HEADS, MOBA_BLOCK, 128),
      vt.reshape(bsz, nb, MOBA_HEADS, MOBA_VT_ROWS, MOBA_BLOCK),
      jnp.transpose(km.reshape(bsz, nb, MOBA_HEADS, 128), (2, 0, 1, 3)))
    return out.reshape(bsz * t, MOBA_W)


def _merge_kernel(x_ref, yg_ref, ym_ref, yl_ref, g0_ref, g1_ref, g2_ref, gb_ref,
                  wg_ref, wm_ref, wl_ref, wo_ref, o_ref):
    gb = gb_ref[...]
    def gate(g_ref, c):
        return _sigmoid(g_ref[...].astype(F32) + gb[:, c * D_MODEL:(c + 1) * D_MODEL])

    mixed = (gate(g0_ref, 0) * _dot(yg_ref[...], wg_ref[...])
             + gate(g1_ref, 1) * _dot(ym_ref[...], wm_ref[...])
             + gate(g2_ref, 2) * _dot(yl_ref[...], wl_ref[...]))
    o_ref[...] = x_ref[...] + _dot(mixed.astype(BF16), wo_ref[...])


def _merge(x, y_gla, y_moba, y_ml, proj, gate_b, w_g, w_m, w_l, w_o, tm):
    n = x.shape[0]
    row = lambda w: pl.BlockSpec((tm, w), lambda i: (i, 0))
    const = lambda shape: pl.BlockSpec(shape, lambda i: (0,) * len(shape))
    gate = lambda c: pl.BlockSpec((tm, D_MODEL), lambda i: (i, COL_GATE0 + c))
    return pl.pallas_call(
        _merge_kernel,
        grid=(n // tm,),
        in_specs=[row(D_MODEL), row(GLA_W), row(MOBA_W), row(MLSTM_V),
                  gate(0), gate(1), gate(2), const((1, N_BRANCH * D_MODEL)),
                  const((GLA_W, D_MODEL)), const((MOBA_W, D_MODEL)), const((MLSTM_V, D_MODEL)),
                  const((D_MODEL, D_MODEL))],
        out_specs=row(D_MODEL),
        out_shape=jax.ShapeDtypeStruct((n, D_MODEL), F32),
        compiler_params=_params("parallel"),
        name="merge",
    )(x, y_gla, y_moba, y_ml, proj, proj, proj, gate_b, w_g, w_m, w_l, w_o)


def _ffn_kernel(x_ref, g_ref, w1_ref, w2_ref, o_ref, h_ref):
    @pl.when(pl.program_id(1) == 0)
    def _():
        x = x_ref[...]
        ms = jnp.mean(x * x, axis=-1, keepdims=True)
        h_ref[...] = (x * lax.rsqrt(ms + EPS) * g_ref[...]).astype(BF16)
        o_ref[...] = x

    a = jnp.square(jnp.maximum(_dot(h_ref[...], w1_ref[...]), 0.0)).astype(BF16)
    o_ref[...] += _dot(a, w2_ref[...])


def _ffn(x, g, w1, w2, tm, tf):
    n = x.shape[0]
    return pl.pallas_call(
        _ffn_kernel,
        grid=(n // tm, D_FF // tf),
        in_specs=[
            pl.BlockSpec((tm, D_MODEL), lambda i, j: (i, 0)),
            pl.BlockSpec((1, D_MODEL), lambda i, j: (0, 0)),
            pl.BlockSpec((D_MODEL, tf), lambda i, j: (0, j)),
            pl.BlockSpec((tf, D_MODEL), lambda i, j: (j, 0)),
        ],
        out_specs=pl.BlockSpec((tm, D_MODEL), lambda i, j: (i, 0)),
        out_shape=jax.ShapeDtypeStruct((n, D_MODEL), F32),
        scratch_shapes=[pltpu.VMEM((tm, D_MODEL), BF16)],
        compiler_params=_params("parallel", "arbitrary"),
        name="ffn",
    )(x, g, w1, w2)


def _rope_tables(seq):
    half = MOBA_HD // 2
    inv = 1.0 / (ROPE_THETA ** (jnp.arange(0, MOBA_HD, 2, dtype=F32) / MOBA_HD))
    ang = jnp.arange(seq, dtype=F32)[:, None] * inv[None, :]
    cos, sin = jnp.cos(ang), jnp.sin(ang)
    cos_t = jnp.tile(cos, (1, 128 // half))
    sin_t = jnp.tile(jnp.concatenate([-sin, sin], axis=1), (1, 128 // MOBA_HD))
    return cos_t, sin_t


def kernel(x, norm1_g, w_in, gla_a_up, gla_a_b, gla_norm_g, moba_qn_g, moba_kn_g, mlstm_conv_w,
           mlstm_i_b, mlstm_f_b, mlstm_norm_g, gate_b, w_br_gla, w_br_moba, w_br_mlstm, w_out,
           norm2_g, w_ff1, w_ff2):
    bsz, t, d = x.shape
    depth = w_in.shape[0]
    n = bsz * t
    assert d == D_MODEL and t % MOBA_BLOCK == 0
    tm = min(512, n)
    tm_big = min(1024, n)
    tb = min(512, t)

    o_ga = 4 * GLA_W
    o_moba = o_ga + GLA_LOWRANK
    o_li = o_moba + 3 * MOBA_W + 2 * MLSTM_QK + 2 * MLSTM_V
    o_gate = o_li + 2 * MLSTM_HEADS
    w_main = jnp.concatenate([w_in[:, :, :o_ga], w_in[:, :, o_moba:o_li], w_in[:, :, o_gate:]],
                             axis=2).astype(BF16)
    w_small = jnp.concatenate(
        [w_in[:, :, o_ga:o_moba], w_in[:, :, o_li:o_gate],
         jnp.zeros((depth, D_MODEL, SMALL_W - GLA_LOWRANK - 2 * MLSTM_HEADS), F32)],
        axis=2).astype(BF16)
    a_up = jnp.concatenate(
        [gla_a_up, jnp.zeros((depth, SMALL_W - GLA_LOWRANK, GLA_W), F32)], axis=1)
    ml_gate_b = jnp.concatenate(
        [jnp.zeros((depth, SM_LI), F32), mlstm_i_b, mlstm_f_b,
         jnp.zeros((depth, SMALL_W - SM_LF - MLSTM_HEADS), F32)], axis=1)
    qn_g = jnp.tile(moba_qn_g, (1, 128 // MOBA_HD))
    kn_g = jnp.tile(moba_kn_g, (1, 128 // MOBA_HD))
    w_g, w_m, w_l, w_o = (w.astype(BF16) for w in (w_br_gla, w_br_moba, w_br_mlstm, w_out))
    w1, w2 = w_ff1.astype(BF16), w_ff2.astype(BF16)

    cos_t, sin_t = _rope_tables(t)
    grp = jnp.arange(128) // MOBA_HD
    gmean = (grp[:, None] == grp[None, :]).astype(F32) / MOBA_HD
    tril = jnp.tril(jnp.ones((GLA_CHUNK, GLA_CHUNK), BF16))

    xf = x.reshape(n, d)
    for l in range(depth):
        proj, small = _inproj(xf, norm1_g[l][None], w_main[l], w_small[l], tm_big, 1024)
        y_gla, y_ml = _recurrent(proj, small, a_up[l], gla_a_b[l][None], gla_norm_g[l][None],
                                 mlstm_conv_w[l], ml_gate_b[l][None], mlstm_norm_g[l][None],
                                 tril, bsz, t, tb)
        qn, kn, vt, km = _moba_prep(proj, cos_t, sin_t, qn_g[l][None], kn_g[l][None], gmean, t)
        y_moba = _moba_attn(qn, kn, vt, km, bsz, t)
        xf = _merge(xf, y_gla, y_moba, y_ml, proj, gate_b[l][None], w_g[l], w_m[l], w_l[l],
                    w_o[l], tm)
        xf = _ffn(xf, norm2_g[l][None], w1[l], w2[l], tm_big, 1024)
    return xf.reshape(bsz, t, d)
```

```python
import functools

import jax
import jax.numpy as jnp
from jax import lax
from jax.experimental import pallas as pl
from jax.experimental.pallas import tpu as pltpu

F32 = jnp.float32
BF16 = jnp.bfloat16

D_MODEL = 1024
GLA_HEADS, GLA_DK, GLA_DV, GLA_LOWRANK = 4, 128, 128, 16
GLA_GATE_NORM = 16.0
GLA_CHUNK = 256
GLA_SUB = 64
MOBA_HEADS, MOBA_HD, MOBA_BLOCK, MOBA_TOPK = 8, 64, 256, 3
ROPE_THETA = 10000.0
MLSTM_HEADS, MLSTM_DQK, MLSTM_DV = 4, 64, 128
MLSTM_CHUNK = 256
MLSTM_CONV = 4
D_FF = 4 * D_MODEL
N_BRANCH = 3
EPS = 1e-6

GLA_W = GLA_HEADS * GLA_DK
MOBA_W = MOBA_HEADS * MOBA_HD
MLSTM_QK = MLSTM_HEADS * MLSTM_DQK
MLSTM_V = MLSTM_HEADS * MLSTM_DV
MOBA_QSCALE = MOBA_HD ** -0.5 * 1.4426950408889634
MOBA_MASKED = -1e30
MOBA_VT_ROWS = MOBA_HD + 16

COL_GLA_Q, COL_GLA_K, COL_GLA_V, COL_GLA_G = 0, 1, 2, 3
COL_MOBA_Q, COL_MOBA_K, COL_MOBA_V = 4, 5, 6
COL_ML_QK, COL_ML_V, COL_ML_O = 7, 8, 9
COL_GATE0 = 5
N_MAIN = 8192
SM_GA, SM_LI, SM_LF = 0, 16, 20
SMALL_W = 128

VMEM_LIMIT = 48 * 1024 * 1024


def _dot(a, b, precision=None):
    return jnp.dot(a, b, preferred_element_type=F32, precision=precision)


def _dot_nt(a, b):
    return lax.dot_general(a, b, (((1,), (1,)), ((), ())), preferred_element_type=F32)


def _dot_tn(a, b):
    return lax.dot_general(a, b, (((0,), (0,)), ((), ())), preferred_element_type=F32)


def _split3(x):
    hi = x.astype(BF16)
    rest = x - hi.astype(F32)
    mid = rest.astype(BF16)
    lo = (rest - mid.astype(F32)).astype(BF16)
    return hi, mid, lo


def _cumsum_rows(tril_bf16, x):
    hi, mid, lo = _split3(x)
    return _dot(tril_bf16, hi) + _dot(tril_bf16, mid) + _dot(tril_bf16, lo)


def _dot_exact_rhs(x, w_bf16):
    hi, mid, lo = _split3(x)
    return _dot(hi, w_bf16) + _dot(mid, w_bf16) + _dot(lo, w_bf16)


def _sigmoid(x):
    return 1.0 / (1.0 + jnp.exp(-x))


def _log_sigmoid(x):
    return jnp.minimum(x, 0.0) - jnp.log(1.0 + jnp.exp(-jnp.abs(x)))


def _params(*sem):
    return pltpu.CompilerParams(dimension_semantics=sem, vmem_limit_bytes=VMEM_LIMIT)


def _inproj_kernel(x_ref, g_ref, w_ref, ws_ref, out_ref, small_ref, hn_ref):
    @pl.when(pl.program_id(1) == 0)
    def _():
        x = x_ref[...]
        ms = jnp.mean(x * x, axis=-1, keepdims=True)
        hn = (x * lax.rsqrt(ms + EPS) * g_ref[...]).astype(BF16)
        hn_ref[...] = hn
        small_ref[...] = _dot(hn, ws_ref[...])

    out_ref[...] = _dot(hn_ref[...], w_ref[...]).astype(out_ref.dtype)


def _inproj(x, g, w_main, w_small, tm, tn):
    n = x.shape[0]
    return pl.pallas_call(
        _inproj_kernel,
        grid=(n // tm, N_MAIN // tn),
        in_specs=[
            pl.BlockSpec((tm, D_MODEL), lambda i, j: (i, 0)),
            pl.BlockSpec((1, D_MODEL), lambda i, j: (0, 0)),
            pl.BlockSpec((D_MODEL, tn), lambda i, j: (0, j)),
            pl.BlockSpec((D_MODEL, SMALL_W), lambda i, j: (0, 0)),
        ],
        out_specs=[
            pl.BlockSpec((tm, tn), lambda i, j: (i, j)),
            pl.BlockSpec((tm, SMALL_W), lambda i, j: (i, 0)),
        ],
        out_shape=[
            jax.ShapeDtypeStruct((n, N_MAIN), BF16),
            jax.ShapeDtypeStruct((n, SMALL_W), F32),
        ],
        scratch_shapes=[pltpu.VMEM((tm, D_MODEL), BF16)],
        compiler_params=_params("parallel", "arbitrary"),
        name="inproj",
    )(x, g, w_main, w_small)


def _gla_setup(q_ref, k_ref, v_ref, g_ref, small_ref, aup_ref, ab_ref, ng_ref, tril_ref,
               o_ref, st_ref, la_ref):
    L = GLA_CHUNK

    @pl.when(pl.program_id(1) == 0)
    def _():
        st_ref[...] = jnp.zeros_like(st_ref)

    z = (_dot_exact_rhs(small_ref[...], aup_ref[0])
         + _dot(small_ref[...].astype(BF16), aup_ref[1]) + ab_ref[...])
    la_ref[...] = _log_sigmoid(z) * (1.0 / GLA_GATE_NORM)

    tril = tril_ref[...]
    S = GLA_SUB
    nsub = L // S
    scale = GLA_DK ** -0.5
    ng = ng_ref[...]

    heads = range(GLA_HEADS)
    hq = [slice(h * GLA_DK, (h + 1) * GLA_DK) for h in heads]
    hv = [slice(h * GLA_DV, (h + 1) * GLA_DV) for h in heads]
    sub = [slice(a * S, (a + 1) * S) for a in range(nsub)]

    def chunk(rows):
        bcum = _cumsum_rows(tril, la_ref[rows, :])
        yield
        b_last = bcum[L - 1:L, :]
        q = q_ref[rows, :].astype(F32) * scale
        k = k_ref[rows, :].astype(F32)
        v = v_ref[rows, :]
        c = [jnp.zeros((1, GLA_W), F32)] + [bcum[a * S - 1:a * S, :] for a in range(1, nsub)]
        c_rows = jnp.concatenate([jnp.broadcast_to(c[a], (S, GLA_W)) for a in range(nsub)], axis=0)
        q_loc = q * jnp.exp(bcum - c_rows)
        k_loc = (k * jnp.exp(c_rows - bcum)).astype(BF16)
        q_dec = (q * jnp.exp(bcum)).astype(BF16)
        k_end = (k * jnp.exp(b_last - bcum)).astype(BF16)
        decay = jnp.exp(b_last)
        q_vs = [jnp.concatenate(
            [(q_loc[sub[a], :] if a == b else q_loc[sub[a], :] * jnp.exp(c[a] - c[b])).astype(BF16)
             for a in range(b, nsub)], axis=0) for b in range(nsub)]
        st = [st_ref[h] for h in heads]
        yield
        attn = [[_dot_nt(q_vs[b][:, hq[h]], k_loc[sub[b], hq[h]]) for h in heads]
                for b in range(nsub)]
        o_inter = [_dot_nt(q_dec[:, hq[h]], st[h].astype(BF16)) for h in heads]
        kv = [_dot_tn(v[:, hv[h]], k_end[:, hq[h]]) for h in heads]
        yield
        pv = []
        for b in range(nsub):
            shape = (L - b * S, S)
            keep = (lax.broadcasted_iota(jnp.int32, shape, 0)
                    >= lax.broadcasted_iota(jnp.int32, shape, 1))
            pv.append([_dot(jnp.where(keep, attn[b][h], 0.0).astype(BF16), v[sub[b], hv[h]])
                       for h in heads])
        yield
        for h in heads:
            o = jnp.concatenate(
                [o_inter[h][sub[a], :] + sum(pv[b][h][(a - b) * S:(a - b + 1) * S, :]
                                             for b in range(a + 1)) for a in range(nsub)], axis=0)
            st_ref[h] = st[h] * decay[:, hq[h]] + kv[h]
            ms = jnp.mean(o * o, axis=-1, keepdims=True)
            y = o * lax.rsqrt(ms + EPS) * ng
            gg = g_ref[rows, hv[h]].astype(F32)
            o_ref[rows, hv[h]] = (y * (gg * _sigmoid(gg))).astype(o_ref.dtype)

    return chunk


def _mlstm_setup(qk_ref, v_ref, lo_ref, small_ref, cw_ref, gb_ref, ng_ref, tril_ref,
                 out_ref, ext_ref, qkc_ref, gc_ref, st_ref, m_ref):
    tb = qk_ref.shape[0]
    L = MLSTM_CHUNK
    pad = 8

    @pl.when(pl.program_id(1) == 0)
    def _():
        st_ref[...] = jnp.zeros_like(st_ref)
        m_ref[...] = jnp.zeros_like(m_ref)
        ext_ref[0:pad, :] = jnp.zeros((pad, ext_ref.shape[1]), F32)

    x = qk_ref[...].astype(F32)
    ext_ref[pad:pad + tb, :] = x
    acc = None
    for w in range(MLSTM_CONV):
        off = pad - (MLSTM_CONV - 1) + w
        term = ext_ref[off:off + tb, :] * cw_ref[w:w + 1, :]
        acc = term if acc is None else acc + term
    qkc_ref[...] = acc
    ext_ref[0:pad, :] = x[tb - pad:tb, :]

    gsm = small_ref[...] + gb_ref[...]
    lane = lax.broadcasted_iota(jnp.int32, gsm.shape, 1)
    is_f = (lane >= SM_LF) & (lane < SM_LF + MLSTM_HEADS)
    gc_ref[...] = jnp.where(is_f, _log_sigmoid(gsm), gsm)

    tril = tril_ref[...]
    causal_t = (lax.broadcasted_iota(jnp.int32, (L, L), 0)
                <= lax.broadcasted_iota(jnp.int32, (L, L), 1))
    lane_c = lax.broadcasted_iota(jnp.int32, (L, SMALL_W), 1)
    is_f_c = (lane_c >= SM_LF) & (lane_c < SM_LF + MLSTM_HEADS)
    ones_rows = jnp.ones((MLSTM_DV, L), F32)
    ng = ng_ref[...]
    kscale = MLSTM_DQK ** -0.5
    neg = -jnp.inf

    def chunk(rows):
        gates = gc_ref[rows, :]
        cum = _cumsum_rows(tril, gates)
        yield
        pc = jnp.where(is_f_c, cum, gates)
        pr = pc.T
        pcu = pc - pltpu.roll(pc, SMALL_W - (SM_LF - SM_LI), 1)
        heads = range(MLSTM_HEADS)
        hv = [slice(h * MLSTM_DV, (h + 1) * MLSTM_DV) for h in heads]
        qk_all = qkc_ref[rows, :]
        st = [st_ref[h] for h in heads]

        d_t, inter_w, m_row, w_r, m_new, a_scale, c_scale = [], [], [], [], [], [], []
        for h in heads:
            b_r = pr[SM_LF + h:SM_LF + h + 1, :]
            u_r = pr[SM_LI + h:SM_LI + h + 1, :] - b_r
            u_c = pcu[:, SM_LI + h:SM_LI + h + 1]
            b_last = b_r[:, L - 1:L]
            m_prev = m_ref[h:h + 1, 0:1]
            um = jnp.where(causal_t, u_c, neg)
            run_max = jnp.max(um, axis=0, keepdims=True)
            m_row.append(b_r + jnp.maximum(m_prev, run_max))
            d_t.append(jnp.exp(um + (b_r - m_row[h])))
            inter_w.append(jnp.exp(b_r + m_prev - m_row[h]))
            m_loc = jnp.max(u_r, axis=-1, keepdims=True) + b_last
            w_r.append(jnp.exp(u_r + b_last - m_loc))
            m_new.append(jnp.maximum(b_last + m_prev, m_loc))
            a_scale.append(jnp.exp(b_last + m_prev - m_new[h]))
            c_scale.append(jnp.exp(m_loc - m_new[h]))

        yield
        q = [qk_all[:, h * MLSTM_DQK:(h + 1) * MLSTM_DQK].astype(BF16) for h in heads]
        k = [(qk_all[:, MLSTM_QK + h * MLSTM_DQK:MLSTM_QK + (h + 1) * MLSTM_DQK]
              * kscale).astype(BF16) for h in heads]
        v_aug_t = [jnp.concatenate([v_ref[rows, hv[h]].astype(F32).T, ones_rows], axis=0)
                   for h in heads]
        s_t = [_dot_nt(k[h], q[h]) for h in heads]
        inter_t = [_dot_nt(st[h].astype(BF16), q[h]) for h in heads]
        kv_t = [_dot((v_aug_t[h] * w_r[h]).astype(BF16), k[h]) for h in heads]

        yield
        sw = [(s_t[h] * d_t[h]).astype(BF16) for h in heads]
        intra_t = [_dot(v_aug_t[h].astype(BF16), sw[h]) for h in heads]
        yield
        for h in heads:
            tot = intra_t[h] + inter_w[h] * inter_t[h]
            num = tot[:MLSTM_DV, :]
            den = tot[MLSTM_DV:, :]
            hid_t = num / jnp.maximum(jnp.abs(den), jnp.exp(-m_row[h]))
            st_ref[h] = a_scale[h] * st[h] + c_scale[h] * kv_t[h]
            m_ref[h:h + 1, :] = jnp.broadcast_to(m_new[h], (1, m_ref.shape[1]))
            ms = jnp.mean(hid_t * hid_t, axis=0, keepdims=True)
            y = (hid_t * lax.rsqrt(ms + EPS)).T * ng
            out_ref[rows, hv[h]] = (y * _sigmoid(lo_ref[rows, hv[h]].astype(F32))).astype(out_ref.dtype)

    return chunk


def _interleave(*gens):
    live = list(gens)
    while live:
        live = [g for g in live if next(g, True) is None]


def _recurrent_kernel(gq_ref, gk_ref, gv_ref, gg_ref, lqk_ref, lv_ref, lo_ref, small_ref,
                      aup_ref, ab_ref, gng_ref, cw_ref, gb_ref, lng_ref, tril_ref,
                      yg_ref, yl_ref,
                      gst_ref, la_ref, ext_ref, qkc_ref, gc_ref, lst_ref, m_ref):
    assert GLA_CHUNK == MLSTM_CHUNK
    L = GLA_CHUNK
    tb = gq_ref.shape[0]
    gla_chunk = _gla_setup(gq_ref, gk_ref, gv_ref, gg_ref, small_ref, aup_ref, ab_ref, gng_ref,
                           tril_ref, yg_ref, gst_ref, la_ref)
    mlstm_chunk = _mlstm_setup(lqk_ref, lv_ref, lo_ref, small_ref, cw_ref, gb_ref, lng_ref,
                               tril_ref, yl_ref, ext_ref, qkc_ref, gc_ref, lst_ref, m_ref)

    def body(c, carry):
        rows = pl.ds(pl.multiple_of(c * L, L), L)
        _interleave(gla_chunk(rows), mlstm_chunk(rows))
        return carry

    lax.fori_loop(0, tb // L, body, 0)


def _recurrent(proj, small, a_up, a_b, gla_ng, conv_w, gate_bias, ml_ng, tril, bsz, t, tb):
    n = proj.shape[0]
    nt = t // tb
    width = GLA_W
    assert MLSTM_V == width and 2 * MLSTM_QK == width

    def col(cidx):
        return pl.BlockSpec((tb, width), lambda b, s: (b * nt + s, cidx))

    const = lambda shape: pl.BlockSpec(shape, lambda b, s: (0,) * len(shape))
    out = pl.BlockSpec((tb, width), lambda b, s: (b * nt + s, 0))
    return pl.pallas_call(
        _recurrent_kernel,
        grid=(bsz, nt),
        in_specs=[
            col(COL_GLA_Q), col(COL_GLA_K), col(COL_GLA_V), col(COL_GLA_G),
            col(COL_ML_QK), col(COL_ML_V), col(COL_ML_O),
            pl.BlockSpec((tb, SMALL_W), lambda b, s: (b * nt + s, 0)),
            const((2, SMALL_W, GLA_W)), const((1, GLA_W)), const((1, GLA_DV)),
            const((MLSTM_CONV, 2 * MLSTM_QK)), const((1, SMALL_W)), const((1, MLSTM_DV)),
            const((GLA_CHUNK, GLA_CHUNK)),
        ],
        out_specs=[out, out],
        out_shape=[jax.ShapeDtypeStruct((n, width), BF16), jax.ShapeDtypeStruct((n, width), BF16)],
        scratch_shapes=[
            pltpu.VMEM((GLA_HEADS, GLA_DV, GLA_DK), F32),
            pltpu.VMEM((tb, GLA_W), F32),
            pltpu.VMEM((tb + 8, 2 * MLSTM_QK), F32),
            pltpu.VMEM((tb, 2 * MLSTM_QK), F32),
            pltpu.VMEM((tb, SMALL_W), F32),
            pltpu.VMEM((MLSTM_HEADS, 2 * MLSTM_DV, MLSTM_DQK), F32),
            pltpu.VMEM((8, 128), F32),
        ],
        compiler_params=_params("parallel", "arbitrary"),
        name="recurrent",
    )(proj, proj, proj, proj, proj, proj, proj, small, a_up, a_b, gla_ng, conv_w, gate_bias,
      ml_ng, tril)


def _moba_prep_kernel(q_ref, k_ref, v_ref, cos_ref, sin_ref, qg_ref, kg_ref, gm_ref,
                      qo_ref, ko_ref, vt_ref, km_ref, *, tblk):
    cos = cos_ref[...]
    sin = sin_ref[...]
    gm = gm_ref[...]
    lane = lax.broadcasted_iota(jnp.int32, cos.shape, 1)
    first_half = (lane % MOBA_HD) < (MOBA_HD // 2)
    low = lane < MOBA_HD
    j = pl.program_id(0) % tblk
    onehot = jnp.where(lane == MOBA_HD + j, 1.0, 0.0)

    def norm_rope(x, g):
        ms = _dot_exact_rhs(x * x, gm)
        y = x * lax.rsqrt(ms + EPS) * g
        partner = jnp.where(first_half, pltpu.roll(y, 128 - MOBA_HD // 2, 1),
                            pltpu.roll(y, MOBA_HD // 2, 1))
        return y * cos + partner * sin

    for p in range(MOBA_W // 128):
        cs = slice(p * 128, (p + 1) * 128)
        qn = norm_rope(q_ref[:, cs].astype(F32), qg_ref[...]) * MOBA_QSCALE
        kn = norm_rope(k_ref[:, cs].astype(F32), kg_ref[...])
        v_t = v_ref[:, cs].astype(F32).T.astype(BF16)
        for h in range(2):
            hh = 2 * p + h
            qh = qn if h == 0 else pltpu.roll(qn, MOBA_HD, 1)
            kh = kn if h == 0 else pltpu.roll(kn, MOBA_HD, 1)
            qo_ref[:, hh * 128:(hh + 1) * 128] = jnp.where(low, qh, 0.0).astype(BF16)
            ka = jnp.where(low, kh, onehot)
            ko_ref[0, hh] = ka.astype(BF16)
            km_ref[0, hh:hh + 1, :] = jnp.mean(ka, axis=0, keepdims=True)
            vt_ref[0, hh, 0:MOBA_HD, :] = v_t[h * MOBA_HD:(h + 1) * MOBA_HD, :]
            vt_ref[0, hh, MOBA_HD:MOBA_VT_ROWS, :] = jnp.ones(
                (MOBA_VT_ROWS - MOBA_HD, MOBA_BLOCK), BF16)


def _moba_prep(proj, cos_t, sin_t, qg, kg, gmean, t):
    n = proj.shape[0]
    nblk = n // MOBA_BLOCK
    tblk = t // MOBA_BLOCK

    def col(cidx):
        return pl.BlockSpec((MOBA_BLOCK, MOBA_W), lambda i: (i, cidx))

    const = lambda shape: pl.BlockSpec(shape, lambda i: (0,) * len(shape))
    tab = pl.BlockSpec((MOBA_BLOCK, 128), lambda i: (i % tblk, 0))
    return pl.pallas_call(
        functools.partial(_moba_prep_kernel, tblk=tblk),
        grid=(nblk,),
        in_specs=[col(COL_MOBA_Q), col(COL_MOBA_K), col(COL_MOBA_V), tab, tab,
                  const((1, 128)), const((1, 128)), const((128, 128))],
        out_specs=[
            pl.BlockSpec((MOBA_BLOCK, MOBA_HEADS * 128), lambda i: (i, 0)),
            pl.BlockSpec((1, MOBA_HEADS, MOBA_BLOCK, 128), lambda i: (i, 0, 0, 0)),
            pl.BlockSpec((1, MOBA_HEADS, MOBA_VT_ROWS, MOBA_BLOCK), lambda i: (i, 0, 0, 0)),
            pl.BlockSpec((1, MOBA_HEADS, 128), lambda i: (i, 0, 0)),
        ],
        out_shape=[
            jax.ShapeDtypeStruct((n, MOBA_HEADS * 128), BF16),
            jax.ShapeDtypeStruct((nblk, MOBA_HEADS, MOBA_BLOCK, 128), BF16),
            jax.ShapeDtypeStruct((nblk, MOBA_HEADS, MOBA_VT_ROWS, MOBA_BLOCK), BF16),
            jax.ShapeDtypeStruct((nblk, MOBA_HEADS, 128), F32),
        ],
        compiler_params=_params("parallel"),
        name="moba_prep",
    )(proj, proj, proj, cos_t, sin_t, qg, kg, gmean)


def _moba_attn_kernel(q_ref, k_ref, vt_ref, km_ref, o_ref,
                      qa_ref, sa_ref, sb_ref, sc_ref, sd_ref, ma_ref, mb_ref, mc_ref, md_ref,
                      m_ref, acc_ref):
    i = pl.program_id(1)
    nbat, nb = k_ref.shape[0], k_ref.shape[1]
    B = MOBA_BLOCK
    blk = lax.broadcasted_iota(jnp.int32, (nb, B), 0)
    key_pos = lax.broadcasted_iota(jnp.int32, (B, B), 0)
    qry_pos = lax.broadcasted_iota(jnp.int32, (B, B), 1)
    neg = -jnp.inf
    streams = [(bb, h) for bb in range(nbat) for h in range(2)]

    def q_slab(bb, h):
        return q_ref[bb, :, h * 128:(h + 1) * 128]

    gates = [_dot_nt(km_ref[h, bb].astype(BF16), q_slab(bb, h)) for bb, h in streams]
    for st, (bb, h) in enumerate(streams):
        g = jnp.where(blk < i, gates[st], neg)
        bias_t = jnp.full((nb, B), MOBA_MASKED, F32)
        for _ in range(MOBA_TOPK):
            mx = jnp.max(g, axis=0, keepdims=True)
            first = jnp.min(jnp.where(g == mx, blk, nb), axis=0, keepdims=True)
            pick = (blk == first) & (mx > neg)
            bias_t = jnp.where(pick, 0.0, bias_t)
            g = jnp.where(pick, neg, g)
        slab_t = jnp.concatenate(
            [jnp.zeros((MOBA_HD, B), F32), bias_t, jnp.zeros((128 - MOBA_HD - nb, B), F32)], axis=0)
        qa_ref[st] = (q_slab(bb, h).astype(F32) + slab_t.T).astype(BF16)
        m_ref[st] = jnp.full((1, B), neg, F32)
        acc_ref[st] = jnp.zeros((MOBA_VT_ROWS, B), F32)

    def stage(st, s, slot):
        s_ref, mx_ref = slot
        s_ref[st] = s
        mx_ref[st] = jnp.max(s, axis=0, keepdims=True)

    def scores_own(slot):
        raw = [_dot_nt(k_ref[bb, i, h], q_slab(bb, h)) for bb, h in streams]
        for st in range(len(streams)):
            stage(st, jnp.where(key_pos <= qry_pos, raw[st], neg), slot)

    def scores(j, slot):
        raw = [_dot_nt(k_ref[bb, j, h], qa_ref[st]) for st, (bb, h) in enumerate(streams)]
        for st in range(len(streams)):
            stage(st, raw[st], slot)

    def accumulate(j, slot):
        s_ref, mx_ref = slot
        for st, (bb, h) in enumerate(streams):
            m_old = m_ref[st]
            m_new = jnp.maximum(m_old, mx_ref[st])
            alpha = jnp.exp2(m_old - m_new)
            p = jnp.exp2((s_ref[st] - m_new).astype(BF16))
            m_ref[st] = m_new
            acc_ref[st] = alpha * acc_ref[st] + _dot(vt_ref[bb, j, h], p)

    def block_at(n):
        return jnp.minimum(n - 1, i)

    slot_a, slot_b, slot_c, slot_d = ((sa_ref, ma_ref), (sb_ref, mb_ref),
                                      (sc_ref, mc_ref), (sd_ref, md_ref))
    scores_own(slot_a)
    scores(block_at(1), slot_b)

    def body(u, carry):
        n0 = 4 * u
        scores(block_at(n0 + 2), slot_c)
        accumulate(jnp.where(u == 0, i, block_at(n0)), slot_a)
        scores(block_at(n0 + 3), slot_d)
        accumulate(block_at(n0 + 1), slot_b)
        scores(block_at(n0 + 4), slot_a)
        accumulate(block_at(n0 + 2), slot_c)
        scores(block_at(n0 + 5), slot_b)
        accumulate(block_at(n0 + 3), slot_d)
        return carry

    lax.fori_loop(0, lax.shift_right_logical(i + 4, 2), body, 0)

    for bb in range(nbat):
        out_t = jnp.concatenate(
            [acc_ref[2 * bb + h, 0:MOBA_HD, :] / acc_ref[2 * bb + h, MOBA_HD:MOBA_HD + 1, :]
             for h in range(2)], axis=0)
        o_ref[bb] = out_t.T.astype(o_ref.dtype)


def _moba_attn(qn, kn, vt, km, bsz, t):
    nb = t // MOBA_BLOCK
    npair = MOBA_W // 128
    ns = 2 * bsz
    scores_slot = pltpu.VMEM((ns, MOBA_BLOCK, MOBA_BLOCK), F32)
    row_slot = pltpu.VMEM((ns, 1, MOBA_BLOCK), F32)
    out = pl.pallas_call(
        _moba_attn_kernel,
        grid=(npair, nb),
        in_specs=[
            pl.BlockSpec((bsz, MOBA_BLOCK, 256), lambda p, i: (0, i, p)),
            pl.BlockSpec((bsz, nb, 2, MOBA_BLOCK, 128), lambda p, i: (0, 0, p, 0, 0)),
            pl.BlockSpec((bsz, nb, 2, MOBA_VT_ROWS, MOBA_BLOCK), lambda p, i: (0, 0, p, 0, 0)),
            pl.BlockSpec((2, bsz, nb, 128), lambda p, i: (p, 0, 0, 0)),
        ],
        out_specs=pl.BlockSpec((bsz, MOBA_BLOCK, 128), lambda p, i: (0, i, p)),
        out_shape=jax.ShapeDtypeStruct((bsz, t, MOBA_W), BF16),
        scratch_shapes=[
            pltpu.VMEM((ns, MOBA_BLOCK, 128), BF16),
            scores_slot, scores_slot, scores_slot, scores_slot,
            row_slot, row_slot, row_slot, row_slot,
            row_slot,
            pltpu.VMEM((ns, MOBA_VT_ROWS, MOBA_BLOCK), F32),
        ],
        compiler_params=_params("parallel", "arbitrary"),
        name="moba_attn",
    )(qn.reshape(bsz, t, MOBA_HEADS * 128),
      kn.reshape(bsz, nb, MOBA_HEADS, MOBA_BLOCK, 128),
      vt.reshape(bsz, nb, MOBA_HEADS, MOBA_VT_ROWS, MOBA_BLOCK),
      jnp.transpose(km.reshape(bsz, nb, MOBA_HEADS, 128), (2, 0, 1, 3)))
    return out.reshape(bsz * t, MOBA_W)


def _merge_kernel(x_ref, yg_ref, ym_ref, yl_ref, g0_ref, g1_ref, g2_ref, gb_ref,
                  wg_ref, wm_ref, wl_ref, wo_ref, o_ref):
    gb = gb_ref[...]
    def gate(g_ref, c):
        return _sigmoid(g_ref[...].astype(F32) + gb[:, c * D_MODEL:(c + 1) * D_MODEL])

    mixed = (gate(g0_ref, 0) * _dot(yg_ref[...], wg_ref[...])
             + gate(g1_ref, 1) * _dot(ym_ref[...], wm_ref[...])
             + gate(g2_ref, 2) * _dot(yl_ref[...], wl_ref[...]))
    o_ref[...] = x_ref[...] + _dot(mixed.astype(BF16), wo_ref[...])


def _merge(x, y_gla, y_moba, y_ml, proj, gate_b, w_g, w_m, w_l, w_o, tm):
    n = x.shape[0]
    row = lambda w: pl.BlockSpec((tm, w), lambda i: (i, 0))
    const = lambda shape: pl.BlockSpec(shape, lambda i: (0,) * len(shape))
    gate = lambda c: pl.BlockSpec((tm, D_MODEL), lambda i: (i, COL_GATE0 + c))
    return pl.pallas_call(
        _merge_kernel,
        grid=(n // tm,),
        in_specs=[row(D_MODEL), row(GLA_W), row(MOBA_W), row(MLSTM_V),
                  gate(0), gate(1), gate(2), const((1, N_BRANCH * D_MODEL)),
                  const((GLA_W, D_MODEL)), const((MOBA_W, D_MODEL)), const((MLSTM_V, D_MODEL)),
                  const((D_MODEL, D_MODEL))],
        out_specs=row(D_MODEL),
        out_shape=jax.ShapeDtypeStruct((n, D_MODEL), F32),
        compiler_params=_params("parallel"),
        name="merge",
    )(x, y_gla, y_moba, y_ml, proj, proj, proj, gate_b, w_g, w_m, w_l, w_o)


def _ffn_kernel(x_ref, g_ref, w1_ref, w2_ref, o_ref, h_ref):
    @pl.when(pl.program_id(1) == 0)
    def _():
        x = x_ref[...]
        ms = jnp.mean(x * x, axis=-1, keepdims=True)
        h_ref[...] = (x * lax.rsqrt(ms + EPS) * g_ref[...]).astype(BF16)
        o_ref[...] = x

    a = jnp.square(jnp.maximum(_dot(h_ref[...], w1_ref[...]), 0.0)).astype(BF16)
    o_ref[...] += _dot(a, w2_ref[...])


def _ffn(x, g, w1, w2, tm, tf):
    n = x.shape[0]
    return pl.pallas_call(
        _ffn_kernel,
        grid=(n // tm, D_FF // tf),
        in_specs=[
            pl.BlockSpec((tm, D_MODEL), lambda i, j: (i, 0)),
            pl.BlockSpec((1, D_MODEL), lambda i, j: (0, 0)),
            pl.BlockSpec((D_MODEL, tf), lambda i, j: (0, j)),
            pl.BlockSpec((tf, D_MODEL), lambda i, j: (j, 0)),
        ],
        out_specs=pl.BlockSpec((tm, D_MODEL), lambda i, j: (i, 0)),
        out_shape=jax.ShapeDtypeStruct((n, D_MODEL), F32),
        scratch_shapes=[pltpu.VMEM((tm, D_MODEL), BF16)],
        compiler_params=_params("parallel", "arbitrary"),
        name="ffn",
    )(x, g, w1, w2)


def _rope_tables(seq):
    half = MOBA_HD // 2
    inv = 1.0 / (ROPE_THETA ** (jnp.arange(0, MOBA_HD, 2, dtype=F32) / MOBA_HD))
    ang = jnp.arange(seq, dtype=F32)[:, None] * inv[None, :]
    cos, sin = jnp.cos(ang), jnp.sin(ang)
    cos_t = jnp.tile(cos, (1, 128 // half))
    sin_t = jnp.tile(jnp.concatenate([-sin, sin], axis=1), (1, 128 // MOBA_HD))
    return cos_t, sin_t


def kernel(x, norm1_g, w_in, gla_a_up, gla_a_b, gla_norm_g, moba_qn_g, moba_kn_g, mlstm_conv_w,
           mlstm_i_b, mlstm_f_b, mlstm_norm_g, gate_b, w_br_gla, w_br_moba, w_br_mlstm, w_out,
           norm2_g, w_ff1, w_ff2):
    bsz, t, d = x.shape
    depth = w_in.shape[0]
    n = bsz * t
    assert d == D_MODEL and t % MOBA_BLOCK == 0
    tm = min(512, n)
    tm_big = min(1024, n)
    tb = min(512, t)

    o_ga = 4 * GLA_W
    o_moba = o_ga + GLA_LOWRANK
    o_li = o_moba + 3 * MOBA_W + 2 * MLSTM_QK + 2 * MLSTM_V
    o_gate = o_li + 2 * MLSTM_HEADS
    w_main = jnp.concatenate([w_in[:, :, :o_ga], w_in[:, :, o_moba:o_li], w_in[:, :, o_gate:]],
                             axis=2).astype(BF16)
    w_small = jnp.concatenate(
        [w_in[:, :, o_ga:o_moba], w_in[:, :, o_li:o_gate],
         jnp.zeros((depth, D_MODEL, SMALL_W - GLA_LOWRANK - 2 * MLSTM_HEADS), F32)],
        axis=2).astype(BF16)
    a_up = jnp.concatenate(
        [gla_a_up, jnp.zeros((depth, SMALL_W - GLA_LOWRANK, GLA_W), F32)], axis=1)
    a_up_hi = a_up.astype(BF16)
    a_up = jnp.stack([a_up_hi, (a_up - a_up_hi.astype(F32)).astype(BF16)], axis=1)
    ml_gate_b = jnp.concatenate(
        [jnp.zeros((depth, SM_LI), F32), mlstm_i_b, mlstm_f_b,
         jnp.zeros((depth, SMALL_W - SM_LF - MLSTM_HEADS), F32)], axis=1)
    qn_g = jnp.tile(moba_qn_g, (1, 128 // MOBA_HD))
    kn_g = jnp.tile(moba_kn_g, (1, 128 // MOBA_HD))
    w_g, w_m, w_l, w_o = (w.astype(BF16) for w in (w_br_gla, w_br_moba, w_br_mlstm, w_out))
    w1, w2 = w_ff1.astype(BF16), w_ff2.astype(BF16)

    cos_t, sin_t = _rope_tables(t)
    grp = jnp.arange(128) // MOBA_HD
    gmean = ((grp[:, None] == grp[None, :]).astype(F32) / MOBA_HD).astype(BF16)
    tril = jnp.tril(jnp.ones((GLA_CHUNK, GLA_CHUNK), BF16))

    xf = x.reshape(n, d)
    for l in range(depth):
        proj, small = _inproj(xf, norm1_g[l][None], w_main[l], w_small[l], tm_big, 1024)
        y_gla, y_ml = _recurrent(proj, small, a_up[l], gla_a_b[l][None], gla_norm_g[l][None],
                                 mlstm_conv_w[l], ml_gate_b[l][None], mlstm_norm_g[l][None],
                                 tril, bsz, t, tb)
        qn, kn, vt, km = _moba_prep(proj, cos_t, sin_t, qn_g[l][None], kn_g[l][None], gmean, t)
        y_moba = _moba_attn(qn, kn, vt, km, bsz, t)
        xf = _merge(xf, y_gla, y_moba, y_ml, proj, gate_b[l][None], w_g[l], w_m[l], w_l[l],
                    w_o[l], tm)
        xf = _ffn(xf, norm2_g[l][None], w1[l], w2[l], tm_big, 1024)
    return xf.reshape(bsz, t, d)
```

```python
import functools

import jax
import jax.numpy as jnp
from jax import lax
from jax.experimental import pallas as pl
from jax.experimental.pallas import tpu as pltpu

F32 = jnp.float32
BF16 = jnp.bfloat16

D_MODEL = 1024
GLA_HEADS, GLA_DK, GLA_DV, GLA_LOWRANK = 4, 128, 128, 16
GLA_GATE_NORM = 16.0
GLA_CHUNK = 256
GLA_SUB = 64
MOBA_HEADS, MOBA_HD, MOBA_BLOCK, MOBA_TOPK = 8, 64, 256, 3
ROPE_THETA = 10000.0
MLSTM_HEADS, MLSTM_DQK, MLSTM_DV = 4, 64, 128
MLSTM_CHUNK = 256
MLSTM_CONV = 4
D_FF = 4 * D_MODEL
N_BRANCH = 3
EPS = 1e-6

GLA_W = GLA_HEADS * GLA_DK
MOBA_W = MOBA_HEADS * MOBA_HD
MLSTM_QK = MLSTM_HEADS * MLSTM_DQK
MLSTM_V = MLSTM_HEADS * MLSTM_DV
MOBA_QSCALE = MOBA_HD ** -0.5 * 1.4426950408889634
MOBA_MASKED = -1e30
MOBA_VT_ROWS = MOBA_HD + 16

COL_GLA_Q, COL_GLA_K, COL_GLA_V, COL_GLA_G = 0, 1, 2, 3
COL_MOBA_Q, COL_MOBA_K, COL_MOBA_V = 4, 5, 6
COL_ML_QK, COL_ML_V, COL_ML_O = 7, 8, 9
COL_GATE0 = 5
N_MAIN = 8192
SM_GA, SM_LI, SM_LF = 0, 16, 20
SMALL_W = 128

VMEM_LIMIT = 48 * 1024 * 1024


def _dot(a, b, precision=None):
    return jnp.dot(a, b, preferred_element_type=F32, precision=precision)


def _dot_nt(a, b):
    return lax.dot_general(a, b, (((1,), (1,)), ((), ())), preferred_element_type=F32)


def _dot_tn(a, b):
    return lax.dot_general(a, b, (((0,), (0,)), ((), ())), preferred_element_type=F32)


def _split2(x):
    hi = x.astype(BF16)
    lo = (x - hi.astype(F32)).astype(BF16)
    return hi, lo


def _cumsum_rows(tril_bf16, x):
    hi, lo = _split2(x)
    return _dot(tril_bf16, hi) + _dot(tril_bf16, lo)


def _dot_exact_rhs(x, w_bf16):
    hi, lo = _split2(x)
    return _dot(hi, w_bf16) + _dot(lo, w_bf16)


def _sigmoid(x):
    return 1.0 / (1.0 + jnp.exp(-x))


def _log_sigmoid(x):
    return jnp.minimum(x, 0.0) - jnp.log(1.0 + jnp.exp(-jnp.abs(x)))


def _params(*sem):
    return pltpu.CompilerParams(dimension_semantics=sem, vmem_limit_bytes=VMEM_LIMIT)


def _inproj_kernel(x_ref, g_ref, w_ref, ws_ref, out_ref, small_ref, hn_ref):
    @pl.when(pl.program_id(1) == 0)
    def _():
        x = x_ref[...]
        ms = jnp.mean(x * x, axis=-1, keepdims=True)
        hn = (x * lax.rsqrt(ms + EPS) * g_ref[...]).astype(BF16)
        hn_ref[...] = hn
        small_ref[...] = _dot(hn, ws_ref[...])

    out_ref[...] = _dot(hn_ref[...], w_ref[...]).astype(out_ref.dtype)


def _inproj(x, g, w_main, w_small, tm, tn):
    n = x.shape[0]
    return pl.pallas_call(
        _inproj_kernel,
        grid=(n // tm, N_MAIN // tn),
        in_specs=[
            pl.BlockSpec((tm, D_MODEL), lambda i, j: (i, 0)),
            pl.BlockSpec((1, D_MODEL), lambda i, j: (0, 0)),
            pl.BlockSpec((D_MODEL, tn), lambda i, j: (0, j)),
            pl.BlockSpec((D_MODEL, SMALL_W), lambda i, j: (0, 0)),
        ],
        out_specs=[
            pl.BlockSpec((tm, tn), lambda i, j: (i, j)),
            pl.BlockSpec((tm, SMALL_W), lambda i, j: (i, 0)),
        ],
        out_shape=[
            jax.ShapeDtypeStruct((n, N_MAIN), BF16),
            jax.ShapeDtypeStruct((n, SMALL_W), F32),
        ],
        scratch_shapes=[pltpu.VMEM((tm, D_MODEL), BF16)],
        compiler_params=_params("parallel", "arbitrary"),
        name="inproj",
    )(x, g, w_main, w_small)


def _gla_setup(q_ref, k_ref, v_ref, g_ref, small_ref, aup_ref, ab_ref, ng_ref, tril_ref,
               o_ref, st_ref, la_ref):
    L = GLA_CHUNK

    @pl.when(pl.program_id(1) == 0)
    def _():
        st_ref[...] = jnp.zeros_like(st_ref)

    z = (_dot_exact_rhs(small_ref[...], aup_ref[0])
         + _dot(small_ref[...].astype(BF16), aup_ref[1]) + ab_ref[...])
    la_ref[...] = _log_sigmoid(z) * (1.0 / GLA_GATE_NORM)

    tril = tril_ref[...]
    S = GLA_SUB
    nsub = L // S
    scale = GLA_DK ** -0.5
    ng = ng_ref[...]

    heads = range(GLA_HEADS)
    hq = [slice(h * GLA_DK, (h + 1) * GLA_DK) for h in heads]
    hv = [slice(h * GLA_DV, (h + 1) * GLA_DV) for h in heads]
    sub = [slice(a * S, (a + 1) * S) for a in range(nsub)]

    def chunk(rows):
        bcum = _cumsum_rows(tril, la_ref[rows, :])
        yield
        b_last = bcum[L - 1:L, :]
        q = q_ref[rows, :].astype(F32) * scale
        k = k_ref[rows, :].astype(F32)
        v = v_ref[rows, :]
        c = [jnp.zeros((1, GLA_W), F32)] + [bcum[a * S - 1:a * S, :] for a in range(1, nsub)]
        c_rows = jnp.concatenate([jnp.broadcast_to(c[a], (S, GLA_W)) for a in range(nsub)], axis=0)
        e_loc = jnp.exp(bcum - c_rows)
        q_loc = q * e_loc
        k_loc32 = k * (1.0 / e_loc)
        k_loc = k_loc32.astype(BF16)
        q_dec = jnp.concatenate([q_loc[sub[a], :] * jnp.exp(c[a]) for a in range(nsub)],
                                axis=0).astype(BF16)
        k_end = jnp.concatenate([k_loc32[sub[a], :] * jnp.exp(b_last - c[a]) for a in range(nsub)],
                                axis=0).astype(BF16)
        decay = jnp.exp(b_last)
        q_vs = [jnp.concatenate(
            [(q_loc[sub[a], :] if a == b else q_loc[sub[a], :] * jnp.exp(c[a] - c[b])).astype(BF16)
             for a in range(b, nsub)], axis=0) for b in range(nsub)]
        st = [st_ref[h] for h in heads]
        yield
        attn = [[_dot_nt(q_vs[b][:, hq[h]], k_loc[sub[b], hq[h]]) for h in heads]
                for b in range(nsub)]
        o_inter = [_dot_nt(q_dec[:, hq[h]], st[h].astype(BF16)) for h in heads]
        kv = [_dot_tn(v[:, hv[h]], k_end[:, hq[h]]) for h in heads]
        yield
        pv = []
        for b in range(nsub):
            shape = (L - b * S, S)
            keep = (lax.broadcasted_iota(jnp.int32, shape, 0)
                    >= lax.broadcasted_iota(jnp.int32, shape, 1))
            pv.append([_dot(jnp.where(keep, attn[b][h], 0.0).astype(BF16), v[sub[b], hv[h]])
                       for h in heads])
        yield
        for h in heads:
            o = jnp.concatenate(
                [o_inter[h][sub[a], :] + sum(pv[b][h][(a - b) * S:(a - b + 1) * S, :]
                                             for b in range(a + 1)) for a in range(nsub)], axis=0)
            st_ref[h] = st[h] * decay[:, hq[h]] + kv[h]
            ms = jnp.mean(o * o, axis=-1, keepdims=True)
            y = o * lax.rsqrt(ms + EPS) * ng
            gg = g_ref[rows, hv[h]].astype(F32)
            o_ref[rows, hv[h]] = (y * (gg * _sigmoid(gg))).astype(o_ref.dtype)

    return chunk


def _mlstm_setup(qk_ref, v_ref, lo_ref, small_ref, cw_ref, gb_ref, ng_ref, tril_ref,
                 out_ref, ext_ref, qkc_ref, gc_ref, st_ref, m_ref):
    tb = qk_ref.shape[0]
    L = MLSTM_CHUNK
    pad = 8

    @pl.when(pl.program_id(1) == 0)
    def _():
        st_ref[...] = jnp.zeros_like(st_ref)
        m_ref[...] = jnp.zeros_like(m_ref)
        ext_ref[0:pad, :] = jnp.zeros((pad, ext_ref.shape[1]), F32)

    x = qk_ref[...].astype(F32)
    ext_ref[pad:pad + tb, :] = x
    acc = None
    for w in range(MLSTM_CONV):
        off = pad - (MLSTM_CONV - 1) + w
        term = ext_ref[off:off + tb, :] * cw_ref[w:w + 1, :]
        acc = term if acc is None else acc + term
    qkc_ref[...] = acc
    ext_ref[0:pad, :] = x[tb - pad:tb, :]

    gsm = small_ref[...] + gb_ref[...]
    lane = lax.broadcasted_iota(jnp.int32, gsm.shape, 1)
    is_f = (lane >= SM_LF) & (lane < SM_LF + MLSTM_HEADS)
    gc_ref[...] = jnp.where(is_f, _log_sigmoid(gsm), gsm)

    tril = tril_ref[...]
    causal_t = (lax.broadcasted_iota(jnp.int32, (L, L), 0)
                <= lax.broadcasted_iota(jnp.int32, (L, L), 1))
    lane_c = lax.broadcasted_iota(jnp.int32, (L, SMALL_W), 1)
    is_f_c = (lane_c >= SM_LF) & (lane_c < SM_LF + MLSTM_HEADS)
    ones_rows = jnp.ones((MLSTM_DV, L), F32)
    ng = ng_ref[...]
    kscale = MLSTM_DQK ** -0.5
    neg = -jnp.inf

    def chunk(rows):
        gates = gc_ref[rows, :]
        cum = _cumsum_rows(tril, gates)
        yield
        pc = jnp.where(is_f_c, cum, gates)
        pr = pc.T
        pcu = pc - pltpu.roll(pc, SMALL_W - (SM_LF - SM_LI), 1)
        heads = range(MLSTM_HEADS)
        hv = [slice(h * MLSTM_DV, (h + 1) * MLSTM_DV) for h in heads]
        qk_all = qkc_ref[rows, :]
        st = [st_ref[h] for h in heads]

        d_t, inter_w, m_row, w_r, m_new, a_scale, c_scale = [], [], [], [], [], [], []
        for h in heads:
            b_r = pr[SM_LF + h:SM_LF + h + 1, :]
            u_r = pr[SM_LI + h:SM_LI + h + 1, :] - b_r
            u_c = pcu[:, SM_LI + h:SM_LI + h + 1]
            b_last = b_r[:, L - 1:L]
            m_prev = m_ref[h:h + 1, 0:1]
            um = jnp.where(causal_t, u_c, neg)
            run_max = jnp.max(um, axis=0, keepdims=True)
            m_row.append(b_r + jnp.maximum(m_prev, run_max))
            d_t.append(jnp.exp(um + (b_r - m_row[h])))
            inter_w.append(jnp.exp(b_r + m_prev - m_row[h]))
            m_loc = jnp.max(u_r, axis=-1, keepdims=True) + b_last
            w_r.append(jnp.exp(u_r + b_last - m_loc))
            m_new.append(jnp.maximum(b_last + m_prev, m_loc))
            a_scale.append(jnp.exp(b_last + m_prev - m_new[h]))
            c_scale.append(jnp.exp(m_loc - m_new[h]))

        yield
        q = [qk_all[:, h * MLSTM_DQK:(h + 1) * MLSTM_DQK].astype(BF16) for h in heads]
        k = [(qk_all[:, MLSTM_QK + h * MLSTM_DQK:MLSTM_QK + (h + 1) * MLSTM_DQK]
              * kscale).astype(BF16) for h in heads]
        v_aug_t = [jnp.concatenate([v_ref[rows, hv[h]].astype(F32).T, ones_rows], axis=0)
                   for h in heads]
        s_t = [_dot_nt(k[h], q[h]) for h in heads]
        inter_t = [_dot_nt(st[h].astype(BF16), q[h]) for h in heads]
        kv_t = [_dot((v_aug_t[h] * w_r[h]).astype(BF16), k[h]) for h in heads]

        yield
        sw = [(s_t[h] * d_t[h]).astype(BF16) for h in heads]
        intra_t = [_dot(v_aug_t[h].astype(BF16), sw[h]) for h in heads]
        yield
        for h in heads:
            tot = intra_t[h] + inter_w[h] * inter_t[h]
            num = tot[:MLSTM_DV, :]
            den = tot[MLSTM_DV:, :]
            hid_t = num / jnp.maximum(jnp.abs(den), jnp.exp(-m_row[h]))
            st_ref[h] = a_scale[h] * st[h] + c_scale[h] * kv_t[h]
            m_ref[h:h + 1, :] = jnp.broadcast_to(m_new[h], (1, m_ref.shape[1]))
            ms = jnp.mean(hid_t * hid_t, axis=0, keepdims=True)
            y = (hid_t * lax.rsqrt(ms + EPS)).T * ng
            out_ref[rows, hv[h]] = (y * _sigmoid(lo_ref[rows, hv[h]].astype(F32))).astype(out_ref.dtype)

    return chunk


def _interleave(*gens):
    live = list(gens)
    while live:
        live = [g for g in live if next(g, True) is None]


def _recurrent_kernel(gq_ref, gk_ref, gv_ref, gg_ref, lqk_ref, lv_ref, lo_ref, small_ref,
                      aup_ref, ab_ref, gng_ref, cw_ref, gb_ref, lng_ref, tril_ref,
                      yg_ref, yl_ref,
                      gst_ref, la_ref, ext_ref, qkc_ref, gc_ref, lst_ref, m_ref):
    assert GLA_CHUNK == MLSTM_CHUNK
    L = GLA_CHUNK
    tb = gq_ref.shape[0]
    gla_chunk = _gla_setup(gq_ref, gk_ref, gv_ref, gg_ref, small_ref, aup_ref, ab_ref, gng_ref,
                           tril_ref, yg_ref, gst_ref, la_ref)
    mlstm_chunk = _mlstm_setup(lqk_ref, lv_ref, lo_ref, small_ref, cw_ref, gb_ref, lng_ref,
                               tril_ref, yl_ref, ext_ref, qkc_ref, gc_ref, lst_ref, m_ref)

    def body(c, carry):
        rows = pl.ds(pl.multiple_of(c * L, L), L)
        _interleave(gla_chunk(rows), mlstm_chunk(rows))
        return carry

    lax.fori_loop(0, tb // L, body, 0)


def _recurrent(proj, small, a_up, a_b, gla_ng, conv_w, gate_bias, ml_ng, tril, bsz, t, tb):
    n = proj.shape[0]
    nt = t // tb
    width = GLA_W
    assert MLSTM_V == width and 2 * MLSTM_QK == width

    def col(cidx):
        return pl.BlockSpec((tb, width), lambda b, s: (b * nt + s, cidx))

    const = lambda shape: pl.BlockSpec(shape, lambda b, s: (0,) * len(shape))
    out = pl.BlockSpec((tb, width), lambda b, s: (b * nt + s, 0))
    return pl.pallas_call(
        _recurrent_kernel,
        grid=(bsz, nt),
        in_specs=[
            col(COL_GLA_Q), col(COL_GLA_K), col(COL_GLA_V), col(COL_GLA_G),
            col(COL_ML_QK), col(COL_ML_V), col(COL_ML_O),
            pl.BlockSpec((tb, SMALL_W), lambda b, s: (b * nt + s, 0)),
            const((2, SMALL_W, GLA_W)), const((1, GLA_W)), const((1, GLA_DV)),
            const((MLSTM_CONV, 2 * MLSTM_QK)), const((1, SMALL_W)), const((1, MLSTM_DV)),
            const((GLA_CHUNK, GLA_CHUNK)),
        ],
        out_specs=[out, out],
        out_shape=[jax.ShapeDtypeStruct((n, width), BF16), jax.ShapeDtypeStruct((n, width), BF16)],
        scratch_shapes=[
            pltpu.VMEM((GLA_HEADS, GLA_DV, GLA_DK), F32),
            pltpu.VMEM((tb, GLA_W), F32),
            pltpu.VMEM((tb + 8, 2 * MLSTM_QK), F32),
            pltpu.VMEM((tb, 2 * MLSTM_QK), F32),
            pltpu.VMEM((tb, SMALL_W), F32),
            pltpu.VMEM((MLSTM_HEADS, 2 * MLSTM_DV, MLSTM_DQK), F32),
            pltpu.VMEM((8, 128), F32),
        ],
        compiler_params=_params("parallel", "arbitrary"),
        name="recurrent",
    )(proj, proj, proj, proj, proj, proj, proj, small, a_up, a_b, gla_ng, conv_w, gate_bias,
      ml_ng, tril)


def _moba_prep_kernel(q_ref, k_ref, v_ref, cos_ref, sin_ref, qg_ref, kg_ref, gm_ref,
                      qo_ref, ko_ref, vt_ref, km_ref, *, tblk):
    cos = cos_ref[...]
    sin = sin_ref[...]
    gm = gm_ref[...]
    lane = lax.broadcasted_iota(jnp.int32, cos.shape, 1)
    first_half = (lane % MOBA_HD) < (MOBA_HD // 2)
    low = lane < MOBA_HD
    j = pl.program_id(0) % tblk
    onehot = jnp.where(lane == MOBA_HD + j, 1.0, 0.0)

    def norm_rope(x, g):
        ms = _dot_exact_rhs(x * x, gm)
        y = x * lax.rsqrt(ms + EPS) * g
        partner = jnp.where(first_half, pltpu.roll(y, 128 - MOBA_HD // 2, 1),
                            pltpu.roll(y, MOBA_HD // 2, 1))
        return y * cos + partner * sin

    for p in range(MOBA_W // 128):
        cs = slice(p * 128, (p + 1) * 128)
        qn = norm_rope(q_ref[:, cs].astype(F32), qg_ref[...]) * MOBA_QSCALE
        kn = norm_rope(k_ref[:, cs].astype(F32), kg_ref[...])
        v_t = v_ref[:, cs].astype(F32).T.astype(BF16)
        for h in range(2):
            hh = 2 * p + h
            qh = qn if h == 0 else pltpu.roll(qn, MOBA_HD, 1)
            kh = kn if h == 0 else pltpu.roll(kn, MOBA_HD, 1)
            qo_ref[:, hh * 128:(hh + 1) * 128] = jnp.where(low, qh, 0.0).astype(BF16)
            ka = jnp.where(low, kh, onehot)
            ko_ref[0, hh] = ka.astype(BF16)
            km_ref[0, hh:hh + 1, :] = jnp.mean(ka, axis=0, keepdims=True)
            vt_ref[0, hh, 0:MOBA_HD, :] = v_t[h * MOBA_HD:(h + 1) * MOBA_HD, :]
            vt_ref[0, hh, MOBA_HD:MOBA_VT_ROWS, :] = jnp.ones(
                (MOBA_VT_ROWS - MOBA_HD, MOBA_BLOCK), BF16)


def _moba_prep(proj, cos_t, sin_t, qg, kg, gmean, t):
    n = proj.shape[0]
    nblk = n // MOBA_BLOCK
    tblk = t // MOBA_BLOCK

    def col(cidx):
        return pl.BlockSpec((MOBA_BLOCK, MOBA_W), lambda i: (i, cidx))

    const = lambda shape: pl.BlockSpec(shape, lambda i: (0,) * len(shape))
    tab = pl.BlockSpec((MOBA_BLOCK, 128), lambda i: (i % tblk, 0))
    return pl.pallas_call(
        functools.partial(_moba_prep_kernel, tblk=tblk),
        grid=(nblk,),
        in_specs=[col(COL_MOBA_Q), col(COL_MOBA_K), col(COL_MOBA_V), tab, tab,
                  const((1, 128)), const((1, 128)), const((128, 128))],
        out_specs=[
            pl.BlockSpec((MOBA_BLOCK, MOBA_HEADS * 128), lambda i: (i, 0)),
            pl.BlockSpec((1, MOBA_HEADS, MOBA_BLOCK, 128), lambda i: (i, 0, 0, 0)),
            pl.BlockSpec((1, MOBA_HEADS, MOBA_VT_ROWS, MOBA_BLOCK), lambda i: (i, 0, 0, 0)),
            pl.BlockSpec((1, MOBA_HEADS, 128), lambda i: (i, 0, 0)),
        ],
        out_shape=[
            jax.ShapeDtypeStruct((n, MOBA_HEADS * 128), BF16),
            jax.ShapeDtypeStruct((nblk, MOBA_HEADS, MOBA_BLOCK, 128), BF16),
            jax.ShapeDtypeStruct((nblk, MOBA_HEADS, MOBA_VT_ROWS, MOBA_BLOCK), BF16),
            jax.ShapeDtypeStruct((nblk, MOBA_HEADS, 128), F32),
        ],
        compiler_params=_params("parallel"),
        name="moba_prep",
    )(proj, proj, proj, cos_t, sin_t, qg, kg, gmean)


def _moba_attn_kernel(q_ref, k_ref, vt_ref, km_ref, o_ref,
                      qa_ref, sa_ref, sb_ref, sc_ref, sd_ref, ma_ref, mb_ref, mc_ref, md_ref,
                      m_ref, acc_ref):
    i = pl.program_id(1)
    nbat, nb = k_ref.shape[0], k_ref.shape[1]
    B = MOBA_BLOCK
    blk = lax.broadcasted_iota(jnp.int32, (nb, B), 0)
    key_pos = lax.broadcasted_iota(jnp.int32, (B, B), 0)
    qry_pos = lax.broadcasted_iota(jnp.int32, (B, B), 1)
    neg = -jnp.inf
    streams = [(bb, h) for bb in range(nbat) for h in range(2)]

    q_t = [q_ref[bb, :, h * 128:(h + 1) * 128].astype(F32).T for bb, h in streams]
    q_tb = [x.astype(BF16) for x in q_t]

    gates = [_dot(km_ref[h, bb].astype(BF16), q_tb[st]) for st, (bb, h) in enumerate(streams)]
    for st in range(len(streams)):
        g = jnp.where(blk < i, gates[st], neg)
        bias_t = jnp.full((nb, B), MOBA_MASKED, F32)
        for _ in range(MOBA_TOPK):
            mx = jnp.max(g, axis=0, keepdims=True)
            first = jnp.min(jnp.where(g == mx, blk, nb), axis=0, keepdims=True)
            pick = (blk == first) & (mx > neg)
            bias_t = jnp.where(pick, 0.0, bias_t)
            g = jnp.where(pick, neg, g)
        slab_t = jnp.concatenate(
            [jnp.zeros((MOBA_HD, B), F32), bias_t, jnp.zeros((128 - MOBA_HD - nb, B), F32)], axis=0)
        qa_ref[st] = (q_t[st] + slab_t).astype(BF16)
        m_ref[st] = jnp.full((1, B), neg, F32)
        acc_ref[st] = jnp.zeros((MOBA_VT_ROWS, B), F32)

    def stage(st, s, slot):
        s_ref, mx_ref = slot
        s_ref[st] = s
        mx_ref[st] = jnp.max(s, axis=0, keepdims=True)

    def scores_own(slot):
        raw = [_dot(k_ref[bb, i, h], q_tb[st]) for st, (bb, h) in enumerate(streams)]
        for st in range(len(streams)):
            stage(st, jnp.where(key_pos <= qry_pos, raw[st], neg), slot)

    def scores(j, slot):
        raw = [_dot(k_ref[bb, j, h], qa_ref[st]) for st, (bb, h) in enumerate(streams)]
        for st in range(len(streams)):
            stage(st, raw[st], slot)

    def accumulate(j, slot):
        s_ref, mx_ref = slot
        for st, (bb, h) in enumerate(streams):
            m_old = m_ref[st]
            m_new = jnp.maximum(m_old, mx_ref[st])
            alpha = jnp.exp2(m_old - m_new)
            p = jnp.exp2((s_ref[st] - m_new).astype(BF16))
            m_ref[st] = m_new
            acc_ref[st] = alpha * acc_ref[st] + _dot(vt_ref[bb, j, h], p)

    def block_at(n):
        return jnp.minimum(n - 1, i)

    def value_block_at(n):
        return jnp.where(n == 0, i, block_at(n))

    slot_a, slot_b, slot_c, slot_d = ((sa_ref, ma_ref), (sb_ref, mb_ref),
                                      (sc_ref, mc_ref), (sd_ref, md_ref))
    scores_own(slot_a)
    scores(block_at(1), slot_b)

    def body(u, carry):
        n0 = 4 * u
        scores(block_at(n0 + 2), slot_c)
        accumulate(value_block_at(n0), slot_a)
        scores(block_at(n0 + 3), slot_d)
        accumulate(block_at(n0 + 1), slot_b)
        scores(block_at(n0 + 4), slot_a)
        accumulate(block_at(n0 + 2), slot_c)
        scores(block_at(n0 + 5), slot_b)
        accumulate(block_at(n0 + 3), slot_d)
        return carry

    trips = lax.shift_right_logical(i + 2, 2)
    lax.fori_loop(0, trips, body, 0)

    @pl.when(lax.rem(i, 4) < 2)
    def _():
        accumulate(value_block_at(4 * trips), slot_a)
        accumulate(block_at(4 * trips + 1), slot_b)

    for bb in range(nbat):
        out_t = jnp.concatenate(
            [acc_ref[2 * bb + h, 0:MOBA_HD, :] / acc_ref[2 * bb + h, MOBA_HD:MOBA_HD + 1, :]
             for h in range(2)], axis=0)
        o_ref[bb] = out_t.T.astype(o_ref.dtype)


def _moba_attn(qn, kn, vt, km, bsz, t):
    nb = t // MOBA_BLOCK
    npair = MOBA_W // 128
    ns = 2 * bsz
    scores_slot = pltpu.VMEM((ns, MOBA_BLOCK, MOBA_BLOCK), F32)
    row_slot = pltpu.VMEM((ns, 1, MOBA_BLOCK), F32)
    out = pl.pallas_call(
        _moba_attn_kernel,
        grid=(npair, nb),
        in_specs=[
            pl.BlockSpec((bsz, MOBA_BLOCK, 256), lambda p, i: (0, i, p)),
            pl.BlockSpec((bsz, nb, 2, MOBA_BLOCK, 128), lambda p, i: (0, 0, p, 0, 0)),
            pl.BlockSpec((bsz, nb, 2, MOBA_VT_ROWS, MOBA_BLOCK), lambda p, i: (0, 0, p, 0, 0)),
            pl.BlockSpec((2, bsz, nb, 128), lambda p, i: (p, 0, 0, 0)),
        ],
        out_specs=pl.BlockSpec((bsz, MOBA_BLOCK, 128), lambda p, i: (0, i, p)),
        out_shape=jax.ShapeDtypeStruct((bsz, t, MOBA_W), BF16),
        scratch_shapes=[
            pltpu.VMEM((ns, 128, MOBA_BLOCK), BF16),
            scores_slot, scores_slot, scores_slot, scores_slot,
            row_slot, row_slot, row_slot, row_slot,
            row_slot,
            pltpu.VMEM((ns, MOBA_VT_ROWS, MOBA_BLOCK), F32),
        ],
        compiler_params=_params("parallel", "arbitrary"),
        name="moba_attn",
    )(qn.reshape(bsz, t, MOBA_HEADS * 128),
      kn.reshape(bsz, nb, MOBA_HEADS, MOBA_BLOCK, 128),
      vt.reshape(bsz, nb, MOBA_HEADS, MOBA_VT_ROWS, MOBA_BLOCK),
      jnp.transpose(km.reshape(bsz, nb, MOBA_HEADS, 128), (2, 0, 1, 3)))
    return out.reshape(bsz * t, MOBA_W)


def _merge_kernel(x_ref, yg_ref, ym_ref, yl_ref, g0_ref, g1_ref, g2_ref, gb_ref,
                  wg_ref, wm_ref, wl_ref, wo_ref, o_ref):
    gb = gb_ref[...]
    def gate(g_ref, c):
        return _sigmoid(g_ref[...].astype(F32) + gb[:, c * D_MODEL:(c + 1) * D_MODEL])

    mixed = (gate(g0_ref, 0) * _dot(yg_ref[...], wg_ref[...])
             + gate(g1_ref, 1) * _dot(ym_ref[...], wm_ref[...])
             + gate(g2_ref, 2) * _dot(yl_ref[...], wl_ref[...]))
    o_ref[...] = x_ref[...] + _dot(mixed.astype(BF16), wo_ref[...])


def _merge(x, y_gla, y_moba, y_ml, proj, gate_b, w_g, w_m, w_l, w_o, tm):
    n = x.shape[0]
    row = lambda w: pl.BlockSpec((tm, w), lambda i: (i, 0))
    const = lambda shape: pl.BlockSpec(shape, lambda i: (0,) * len(shape))
    gate = lambda c: pl.BlockSpec((tm, D_MODEL), lambda i: (i, COL_GATE0 + c))
    return pl.pallas_call(
        _merge_kernel,
        grid=(n // tm,),
        in_specs=[row(D_MODEL), row(GLA_W), row(MOBA_W), row(MLSTM_V),
                  gate(0), gate(1), gate(2), const((1, N_BRANCH * D_MODEL)),
                  const((GLA_W, D_MODEL)), const((MOBA_W, D_MODEL)), const((MLSTM_V, D_MODEL)),
                  const((D_MODEL, D_MODEL))],
        out_specs=row(D_MODEL),
        out_shape=jax.ShapeDtypeStruct((n, D_MODEL), F32),
        compiler_params=_params("parallel"),
        name="merge",
    )(x, y_gla, y_moba, y_ml, proj, proj, proj, gate_b, w_g, w_m, w_l, w_o)


def _ffn_kernel(x_ref, g_ref, w1_ref, w2_ref, o_ref, h_ref):
    @pl.when(pl.program_id(1) == 0)
    def _():
        x = x_ref[...]
        ms = jnp.mean(x * x, axis=-1, keepdims=True)
        h_ref[...] = (x * lax.rsqrt(ms + EPS) * g_ref[...]).astype(BF16)
        o_ref[...] = x

    a = jnp.square(jnp.maximum(_dot(h_ref[...], w1_ref[...]), 0.0)).astype(BF16)
    o_ref[...] += _dot(a, w2_ref[...])


def _ffn(x, g, w1, w2, tm, tf):
    n = x.shape[0]
    return pl.pallas_call(
        _ffn_kernel,
        grid=(n // tm, D_FF // tf),
        in_specs=[
            pl.BlockSpec((tm, D_MODEL), lambda i, j: (i, 0)),
            pl.BlockSpec((1, D_MODEL), lambda i, j: (0, 0)),
            pl.BlockSpec((D_MODEL, tf), lambda i, j: (0, j)),
            pl.BlockSpec((tf, D_MODEL), lambda i, j: (j, 0)),
        ],
        out_specs=pl.BlockSpec((tm, D_MODEL), lambda i, j: (i, 0)),
        out_shape=jax.ShapeDtypeStruct((n, D_MODEL), F32),
        scratch_shapes=[pltpu.VMEM((tm, D_MODEL), BF16)],
        compiler_params=_params("parallel", "arbitrary"),
        name="ffn",
    )(x, g, w1, w2)


def _rope_tables(seq):
    half = MOBA_HD // 2
    inv = 1.0 / (ROPE_THETA ** (jnp.arange(0, MOBA_HD, 2, dtype=F32) / MOBA_HD))
    ang = jnp.arange(seq, dtype=F32)[:, None] * inv[None, :]
    cos, sin = jnp.cos(ang), jnp.sin(ang)
    cos_t = jnp.tile(cos, (1, 128 // half))
    sin_t = jnp.tile(jnp.concatenate([-sin, sin], axis=1), (1, 128 // MOBA_HD))
    return cos_t, sin_t


def kernel(x, norm1_g, w_in, gla_a_up, gla_a_b, gla_norm_g, moba_qn_g, moba_kn_g, mlstm_conv_w,
           mlstm_i_b, mlstm_f_b, mlstm_norm_g, gate_b, w_br_gla, w_br_moba, w_br_mlstm, w_out,
           norm2_g, w_ff1, w_ff2):
    bsz, t, d = x.shape
    depth = w_in.shape[0]
    n = bsz * t
    assert d == D_MODEL and t % MOBA_BLOCK == 0
    tm = min(512, n)
    tm_big = min(1024, n)
    tb = min(512, t)

    o_ga = 4 * GLA_W
    o_moba = o_ga + GLA_LOWRANK
    o_li = o_moba + 3 * MOBA_W + 2 * MLSTM_QK + 2 * MLSTM_V
    o_gate = o_li + 2 * MLSTM_HEADS
    w_main = jnp.concatenate([w_in[:, :, :o_ga], w_in[:, :, o_moba:o_li], w_in[:, :, o_gate:]],
                             axis=2).astype(BF16)
    w_small = jnp.concatenate(
        [w_in[:, :, o_ga:o_moba], w_in[:, :, o_li:o_gate],
         jnp.zeros((depth, D_MODEL, SMALL_W - GLA_LOWRANK - 2 * MLSTM_HEADS), F32)],
        axis=2).astype(BF16)
    a_up = jnp.concatenate(
        [gla_a_up, jnp.zeros((depth, SMALL_W - GLA_LOWRANK, GLA_W), F32)], axis=1)
    a_up_hi = a_up.astype(BF16)
    a_up = jnp.stack([a_up_hi, (a_up - a_up_hi.astype(F32)).astype(BF16)], axis=1)
    ml_gate_b = jnp.concatenate(
        [jnp.zeros((depth, SM_LI), F32), mlstm_i_b, mlstm_f_b,
         jnp.zeros((depth, SMALL_W - SM_LF - MLSTM_HEADS), F32)], axis=1)
    qn_g = jnp.tile(moba_qn_g, (1, 128 // MOBA_HD))
    kn_g = jnp.tile(moba_kn_g, (1, 128 // MOBA_HD))
    w_g, w_m, w_l, w_o = (w.astype(BF16) for w in (w_br_gla, w_br_moba, w_br_mlstm, w_out))
    w1, w2 = w_ff1.astype(BF16), w_ff2.astype(BF16)

    cos_t, sin_t = _rope_tables(t)
    grp = jnp.arange(128) // MOBA_HD
    gmean = ((grp[:, None] == grp[None, :]).astype(F32) / MOBA_HD).astype(BF16)
    tril = jnp.tril(jnp.ones((GLA_CHUNK, GLA_CHUNK), BF16))

    xf = x.reshape(n, d)
    for l in range(depth):
        proj, small = _inproj(xf, norm1_g[l][None], w_main[l], w_small[l], tm_big, 1024)
        y_gla, y_ml = _recurrent(proj, small, a_up[l], gla_a_b[l][None], gla_norm_g[l][None],
                                 mlstm_conv_w[l], ml_gate_b[l][None], mlstm_norm_g[l][None],
                                 tril, bsz, t, tb)
        qn, kn, vt, km = _moba_prep(proj, cos_t, sin_t, qn_g[l][None], kn_g[l][None], gmean, t)
        y_moba = _moba_attn(qn, kn, vt, km, bsz, t)
        xf = _merge(xf, y_gla, y_moba, y_ml, proj, gate_b[l][None], w_g[l], w_m[l], w_l[l],
                    w_o[l], tm)
        xf = _ffn(xf, norm2_g[l][None], w1[l], w2[l], tm_big, 1024)
    return xf.reshape(bsz, t, d)
```

```python
import functools

import jax
import jax.numpy as jnp
from jax import lax
from jax.experimental import pallas as pl
from jax.experimental.pallas import tpu as pltpu

F32 = jnp.float32
BF16 = jnp.bfloat16

D_MODEL = 1024
GLA_HEADS, GLA_DK, GLA_DV, GLA_LOWRANK = 4, 128, 128, 16
GLA_GATE_NORM = 16.0
GLA_CHUNK = 256
GLA_SUB = 64
MOBA_HEADS, MOBA_HD, MOBA_BLOCK, MOBA_TOPK = 8, 64, 256, 3
ROPE_THETA = 10000.0
MLSTM_HEADS, MLSTM_DQK, MLSTM_DV = 4, 64, 128
MLSTM_CHUNK = 256
MLSTM_CONV = 4
D_FF = 4 * D_MODEL
N_BRANCH = 3
EPS = 1e-6

GLA_W = GLA_HEADS * GLA_DK
MOBA_W = MOBA_HEADS * MOBA_HD
MLSTM_QK = MLSTM_HEADS * MLSTM_DQK
MLSTM_V = MLSTM_HEADS * MLSTM_DV
MOBA_QSCALE = MOBA_HD ** -0.5 * 1.4426950408889634
MOBA_MASKED = -1e30
MOBA_VT_ROWS = MOBA_HD + 16

COL_GLA_Q, COL_GLA_K, COL_GLA_V, COL_GLA_G = 0, 1, 2, 3
COL_MOBA_Q, COL_MOBA_K, COL_MOBA_V = 4, 5, 6
COL_ML_QK, COL_ML_V, COL_ML_O = 7, 8, 9
COL_GATE0 = 5
N_MAIN = 8192
SM_GA, SM_LI, SM_LF = 0, 16, 20
SMALL_W = 128

VMEM_LIMIT = 48 * 1024 * 1024


def _dot(a, b, precision=None):
    return jnp.dot(a, b, preferred_element_type=F32, precision=precision)


def _dot_nt(a, b):
    return lax.dot_general(a, b, (((1,), (1,)), ((), ())), preferred_element_type=F32)


def _dot_tn(a, b):
    return lax.dot_general(a, b, (((0,), (0,)), ((), ())), preferred_element_type=F32)


def _split2(x):
    hi = x.astype(BF16)
    lo = (x - hi.astype(F32)).astype(BF16)
    return hi, lo


def _cumsum_rows(tril_bf16, x):
    hi, lo = _split2(x)
    return _dot(tril_bf16, hi) + _dot(tril_bf16, lo)


def _dot_exact_rhs(x, w_bf16):
    hi, lo = _split2(x)
    return _dot(hi, w_bf16) + _dot(lo, w_bf16)


def _sigmoid(x):
    return 1.0 / (1.0 + jnp.exp(-x))


def _log_sigmoid(x):
    return jnp.minimum(x, 0.0) - jnp.log(1.0 + jnp.exp(-jnp.abs(x)))


def _params(*sem):
    return pltpu.CompilerParams(dimension_semantics=sem, vmem_limit_bytes=VMEM_LIMIT)


def _inproj_kernel(x_ref, g_ref, w_ref, ws_ref, out_ref, small_ref, hn_ref):
    @pl.when(pl.program_id(1) == 0)
    def _():
        x = x_ref[...]
        ms = jnp.mean(x * x, axis=-1, keepdims=True)
        hn = (x * lax.rsqrt(ms + EPS) * g_ref[...]).astype(BF16)
        hn_ref[...] = hn
        small_ref[...] = _dot(hn, ws_ref[...])

    out_ref[...] = _dot(hn_ref[...], w_ref[...]).astype(out_ref.dtype)


def _inproj(x, g, w_main, w_small, tm, tn):
    n = x.shape[0]
    return pl.pallas_call(
        _inproj_kernel,
        grid=(n // tm, N_MAIN // tn),
        in_specs=[
            pl.BlockSpec((tm, D_MODEL), lambda i, j: (i, 0)),
            pl.BlockSpec((1, D_MODEL), lambda i, j: (0, 0)),
            pl.BlockSpec((D_MODEL, tn), lambda i, j: (0, j)),
            pl.BlockSpec((D_MODEL, SMALL_W), lambda i, j: (0, 0)),
        ],
        out_specs=[
            pl.BlockSpec((tm, tn), lambda i, j: (i, j)),
            pl.BlockSpec((tm, SMALL_W), lambda i, j: (i, 0)),
        ],
        out_shape=[
            jax.ShapeDtypeStruct((n, N_MAIN), BF16),
            jax.ShapeDtypeStruct((n, SMALL_W), F32),
        ],
        scratch_shapes=[pltpu.VMEM((tm, D_MODEL), BF16)],
        compiler_params=_params("parallel", "arbitrary"),
        name="inproj",
    )(x, g, w_main, w_small)


def _gla_setup(q_ref, k_ref, v_ref, g_ref, small_ref, aup_ref, ab_ref, ng_ref, tril_ref,
               o_ref, st_ref, la_ref):
    L = GLA_CHUNK

    @pl.when(pl.program_id(1) == 0)
    def _():
        st_ref[...] = jnp.zeros_like(st_ref)

    z = (_dot_exact_rhs(small_ref[...], aup_ref[0])
         + _dot(small_ref[...].astype(BF16), aup_ref[1]) + ab_ref[...])
    la_ref[...] = _log_sigmoid(z) * (1.0 / GLA_GATE_NORM)

    tril = tril_ref[...]
    S = GLA_SUB
    nsub = L // S
    scale = GLA_DK ** -0.5
    ng = ng_ref[...]

    heads = range(GLA_HEADS)
    hq = [slice(h * GLA_DK, (h + 1) * GLA_DK) for h in heads]
    hv = [slice(h * GLA_DV, (h + 1) * GLA_DV) for h in heads]
    sub = [slice(a * S, (a + 1) * S) for a in range(nsub)]

    def chunk(rows):
        bcum = _cumsum_rows(tril, la_ref[rows, :])
        yield
        b_last = bcum[L - 1:L, :]
        q = q_ref[rows, :].astype(F32) * scale
        k = k_ref[rows, :].astype(F32)
        v = v_ref[rows, :]
        c = [jnp.zeros((1, GLA_W), F32)] + [bcum[a * S - 1:a * S, :] for a in range(1, nsub)]
        c_rows = jnp.concatenate([jnp.broadcast_to(c[a], (S, GLA_W)) for a in range(nsub)], axis=0)
        e_loc = jnp.exp(bcum - c_rows)
        q_loc = q * e_loc
        k_loc32 = k * (1.0 / e_loc)
        k_loc = k_loc32.astype(BF16)
        q_dec = jnp.concatenate([q_loc[sub[a], :] * jnp.exp(c[a]) for a in range(nsub)],
                                axis=0).astype(BF16)
        k_end = jnp.concatenate([k_loc32[sub[a], :] * jnp.exp(b_last - c[a]) for a in range(nsub)],
                                axis=0).astype(BF16)
        decay = jnp.exp(b_last)
        q_vs = [jnp.concatenate(
            [(q_loc[sub[a], :] if a == b else q_loc[sub[a], :] * jnp.exp(c[a] - c[b])).astype(BF16)
             for a in range(b, nsub)], axis=0) for b in range(nsub)]
        st = [st_ref[h] for h in heads]
        yield
        attn = [[_dot_nt(q_vs[b][:, hq[h]], k_loc[sub[b], hq[h]]) for h in heads]
                for b in range(nsub)]
        o_inter = [_dot_nt(q_dec[:, hq[h]], st[h].astype(BF16)) for h in heads]
        kv = [_dot_tn(v[:, hv[h]], k_end[:, hq[h]]) for h in heads]
        yield
        pv = []
        for b in range(nsub):
            shape = (L - b * S, S)
            keep = (lax.broadcasted_iota(jnp.int32, shape, 0)
                    >= lax.broadcasted_iota(jnp.int32, shape, 1))
            pv.append([_dot(jnp.where(keep, attn[b][h], 0.0).astype(BF16), v[sub[b], hv[h]])
                       for h in heads])
        yield
        for h in heads:
            o = jnp.concatenate(
                [o_inter[h][sub[a], :] + sum(pv[b][h][(a - b) * S:(a - b + 1) * S, :]
                                             for b in range(a + 1)) for a in range(nsub)], axis=0)
            st_ref[h] = st[h] * decay[:, hq[h]] + kv[h]
            ms = jnp.mean(o * o, axis=-1, keepdims=True)
            y = o * lax.rsqrt(ms + EPS) * ng
            gg = g_ref[rows, hv[h]].astype(F32)
            o_ref[rows, hv[h]] = (y * (gg * _sigmoid(gg))).astype(o_ref.dtype)

    return chunk


def _mlstm_setup(qk_ref, v_ref, lo_ref, small_ref, cw_ref, gb_ref, ng_ref, tril_ref,
                 out_ref, ext_ref, qkc_ref, gc_ref, st_ref, m_ref):
    tb = qk_ref.shape[0]
    L = MLSTM_CHUNK
    pad = 8

    @pl.when(pl.program_id(1) == 0)
    def _():
        st_ref[...] = jnp.zeros_like(st_ref)
        m_ref[...] = jnp.zeros_like(m_ref)
        ext_ref[0:pad, :] = jnp.zeros((pad, ext_ref.shape[1]), F32)

    x = qk_ref[...].astype(F32)
    ext_ref[pad:pad + tb, :] = x
    acc = None
    for w in range(MLSTM_CONV):
        off = pad - (MLSTM_CONV - 1) + w
        term = ext_ref[off:off + tb, :] * cw_ref[w:w + 1, :]
        acc = term if acc is None else acc + term
    qkc_ref[...] = acc
    ext_ref[0:pad, :] = x[tb - pad:tb, :]

    gsm = small_ref[...] + gb_ref[...]
    lane = lax.broadcasted_iota(jnp.int32, gsm.shape, 1)
    is_f = (lane >= SM_LF) & (lane < SM_LF + MLSTM_HEADS)
    gc_ref[...] = jnp.where(is_f, _log_sigmoid(gsm), gsm)

    tril = tril_ref[...]
    causal_t = (lax.broadcasted_iota(jnp.int32, (L, L), 0)
                <= lax.broadcasted_iota(jnp.int32, (L, L), 1))
    lane_c = lax.broadcasted_iota(jnp.int32, (L, SMALL_W), 1)
    is_f_c = (lane_c >= SM_LF) & (lane_c < SM_LF + MLSTM_HEADS)
    ones_rows = jnp.ones((MLSTM_DV, L), F32)
    ng = ng_ref[...]
    kscale = MLSTM_DQK ** -0.5
    neg = -jnp.inf

    def chunk(rows):
        gates = gc_ref[rows, :]
        cum = _cumsum_rows(tril, gates)
        yield
        pc = jnp.where(is_f_c, cum, gates)
        pr = pc.T
        pcu = pc - pltpu.roll(pc, SMALL_W - (SM_LF - SM_LI), 1)
        heads = range(MLSTM_HEADS)
        hv = [slice(h * MLSTM_DV, (h + 1) * MLSTM_DV) for h in heads]
        qk_all = qkc_ref[rows, :]
        st = [st_ref[h] for h in heads]

        d_t, inter_w, m_row, w_r, m_new, a_scale, c_scale = [], [], [], [], [], [], []
        for h in heads:
            b_r = pr[SM_LF + h:SM_LF + h + 1, :]
            u_r = pr[SM_LI + h:SM_LI + h + 1, :] - b_r
            u_c = pcu[:, SM_LI + h:SM_LI + h + 1]
            b_last = b_r[:, L - 1:L]
            m_prev = m_ref[h:h + 1, 0:1]
            um = jnp.where(causal_t, u_c, neg)
            run_max = jnp.max(um, axis=0, keepdims=True)
            m_row.append(b_r + jnp.maximum(m_prev, run_max))
            d_t.append(jnp.exp(um + (b_r - m_row[h])))
            inter_w.append(jnp.exp(b_r + m_prev - m_row[h]))
            m_loc = jnp.max(u_r, axis=-1, keepdims=True) + b_last
            w_r.append(jnp.exp(u_r + b_last - m_loc))
            m_new.append(jnp.maximum(b_last + m_prev, m_loc))
            a_scale.append(jnp.exp(b_last + m_prev - m_new[h]))
            c_scale.append(jnp.exp(m_loc - m_new[h]))

        yield
        q = [qk_all[:, h * MLSTM_DQK:(h + 1) * MLSTM_DQK].astype(BF16) for h in heads]
        k = [(qk_all[:, MLSTM_QK + h * MLSTM_DQK:MLSTM_QK + (h + 1) * MLSTM_DQK]
              * kscale).astype(BF16) for h in heads]
        v_aug_t = [jnp.concatenate([v_ref[rows, hv[h]].astype(F32).T, ones_rows], axis=0)
                   for h in heads]
        s_t = [_dot_nt(k[h], q[h]) for h in heads]
        inter_t = [_dot_nt(st[h].astype(BF16), q[h]) for h in heads]
        kv_t = [_dot((v_aug_t[h] * w_r[h]).astype(BF16), k[h]) for h in heads]

        yield
        sw = [(s_t[h] * d_t[h]).astype(BF16) for h in heads]
        intra_t = [_dot(v_aug_t[h].astype(BF16), sw[h]) for h in heads]
        yield
        for h in heads:
            tot = intra_t[h] + inter_w[h] * inter_t[h]
            num = tot[:MLSTM_DV, :]
            den = tot[MLSTM_DV:, :]
            hid_t = num / jnp.maximum(jnp.abs(den), jnp.exp(-m_row[h]))
            st_ref[h] = a_scale[h] * st[h] + c_scale[h] * kv_t[h]
            m_ref[h:h + 1, :] = jnp.broadcast_to(m_new[h], (1, m_ref.shape[1]))
            ms = jnp.mean(hid_t * hid_t, axis=0, keepdims=True)
            y = (hid_t * lax.rsqrt(ms + EPS)).T * ng
            out_ref[rows, hv[h]] = (y * _sigmoid(lo_ref[rows, hv[h]].astype(F32))).astype(out_ref.dtype)

    return chunk


def _interleave(*gens):
    live = list(gens)
    while live:
        live = [g for g in live if next(g, True) is None]


def _recurrent_kernel(gq_ref, gk_ref, gv_ref, gg_ref, lqk_ref, lv_ref, lo_ref, small_ref,
                      aup_ref, ab_ref, gng_ref, cw_ref, gb_ref, lng_ref, tril_ref,
                      yg_ref, yl_ref,
                      gst_ref, la_ref, ext_ref, qkc_ref, gc_ref, lst_ref, m_ref):
    assert GLA_CHUNK == MLSTM_CHUNK
    L = GLA_CHUNK
    tb = gq_ref.shape[0]
    gla_chunk = _gla_setup(gq_ref, gk_ref, gv_ref, gg_ref, small_ref, aup_ref, ab_ref, gng_ref,
                           tril_ref, yg_ref, gst_ref, la_ref)
    mlstm_chunk = _mlstm_setup(lqk_ref, lv_ref, lo_ref, small_ref, cw_ref, gb_ref, lng_ref,
                               tril_ref, yl_ref, ext_ref, qkc_ref, gc_ref, lst_ref, m_ref)

    def body(c, carry):
        rows = pl.ds(pl.multiple_of(c * L, L), L)
        _interleave(gla_chunk(rows), mlstm_chunk(rows))
        return carry

    lax.fori_loop(0, tb // L, body, 0)


def _recurrent(proj, small, a_up, a_b, gla_ng, conv_w, gate_bias, ml_ng, tril, bsz, t, tb):
    n = proj.shape[0]
    nt = t // tb
    width = GLA_W
    assert MLSTM_V == width and 2 * MLSTM_QK == width

    def col(cidx):
        return pl.BlockSpec((tb, width), lambda b, s: (b * nt + s, cidx))

    const = lambda shape: pl.BlockSpec(shape, lambda b, s: (0,) * len(shape))
    out = pl.BlockSpec((tb, width), lambda b, s: (b * nt + s, 0))
    return pl.pallas_call(
        _recurrent_kernel,
        grid=(bsz, nt),
        in_specs=[
            col(COL_GLA_Q), col(COL_GLA_K), col(COL_GLA_V), col(COL_GLA_G),
            col(COL_ML_QK), col(COL_ML_V), col(COL_ML_O),
            pl.BlockSpec((tb, SMALL_W), lambda b, s: (b * nt + s, 0)),
            const((2, SMALL_W, GLA_W)), const((1, GLA_W)), const((1, GLA_DV)),
            const((MLSTM_CONV, 2 * MLSTM_QK)), const((1, SMALL_W)), const((1, MLSTM_DV)),
            const((GLA_CHUNK, GLA_CHUNK)),
        ],
        out_specs=[out, out],
        out_shape=[jax.ShapeDtypeStruct((n, width), BF16), jax.ShapeDtypeStruct((n, width), BF16)],
        scratch_shapes=[
            pltpu.VMEM((GLA_HEADS, GLA_DV, GLA_DK), F32),
            pltpu.VMEM((tb, GLA_W), F32),
            pltpu.VMEM((tb + 8, 2 * MLSTM_QK), F32),
            pltpu.VMEM((tb, 2 * MLSTM_QK), F32),
            pltpu.VMEM((tb, SMALL_W), F32),
            pltpu.VMEM((MLSTM_HEADS, 2 * MLSTM_DV, MLSTM_DQK), F32),
            pltpu.VMEM((8, 128), F32),
        ],
        compiler_params=_params("parallel", "arbitrary"),
        name="recurrent",
    )(proj, proj, proj, proj, proj, proj, proj, small, a_up, a_b, gla_ng, conv_w, gate_bias,
      ml_ng, tril)


def _moba_prep_kernel(q_ref, k_ref, v_ref, cos_ref, sin_ref, qg_ref, kg_ref, gm_ref,
                      qo_ref, ko_ref, vt_ref, km_ref, *, tblk):
    cos = cos_ref[...]
    sin = sin_ref[...]
    gm = gm_ref[...]
    lane = lax.broadcasted_iota(jnp.int32, cos.shape, 1)
    first_half = (lane % MOBA_HD) < (MOBA_HD // 2)
    low = lane < MOBA_HD
    j = pl.program_id(0) % tblk
    onehot = jnp.where(lane == MOBA_HD + j, 1.0, 0.0)

    def norm_rope(x, g):
        ms = _dot_exact_rhs(x * x, gm)
        y = x * lax.rsqrt(ms + EPS) * g
        partner = jnp.where(first_half, pltpu.roll(y, 128 - MOBA_HD // 2, 1),
                            pltpu.roll(y, MOBA_HD // 2, 1))
        return y * cos + partner * sin

    for p in range(MOBA_W // 128):
        cs = slice(p * 128, (p + 1) * 128)
        qn = norm_rope(q_ref[:, cs].astype(F32), qg_ref[...]) * MOBA_QSCALE
        kn = norm_rope(k_ref[:, cs].astype(F32), kg_ref[...])
        v_t = v_ref[:, cs].astype(F32).T.astype(BF16)
        for h in range(2):
            hh = 2 * p + h
            qh = qn if h == 0 else pltpu.roll(qn, MOBA_HD, 1)
            kh = kn if h == 0 else pltpu.roll(kn, MOBA_HD, 1)
            qo_ref[:, hh * 128:(hh + 1) * 128] = jnp.where(low, qh, 0.0).astype(BF16)
            ka = jnp.where(low, kh, onehot)
            ko_ref[0, hh] = ka.astype(BF16)
            km_ref[0, hh:hh + 1, :] = jnp.mean(ka, axis=0, keepdims=True)
            vt_ref[0, hh, 0:MOBA_HD, :] = v_t[h * MOBA_HD:(h + 1) * MOBA_HD, :]
            vt_ref[0, hh, MOBA_HD:MOBA_VT_ROWS, :] = jnp.ones(
                (MOBA_VT_ROWS - MOBA_HD, MOBA_BLOCK), BF16)


def _moba_prep(proj, cos_t, sin_t, qg, kg, gmean, t):
    n = proj.shape[0]
    nblk = n // MOBA_BLOCK
    tblk = t // MOBA_BLOCK

    def col(cidx):
        return pl.BlockSpec((MOBA_BLOCK, MOBA_W), lambda i: (i, cidx))

    const = lambda shape: pl.BlockSpec(shape, lambda i: (0,) * len(shape))
    tab = pl.BlockSpec((MOBA_BLOCK, 128), lambda i: (i % tblk, 0))
    return pl.pallas_call(
        functools.partial(_moba_prep_kernel, tblk=tblk),
        grid=(nblk,),
        in_specs=[col(COL_MOBA_Q), col(COL_MOBA_K), col(COL_MOBA_V), tab, tab,
                  const((1, 128)), const((1, 128)), const((128, 128))],
        out_specs=[
            pl.BlockSpec((MOBA_BLOCK, MOBA_HEADS * 128), lambda i: (i, 0)),
            pl.BlockSpec((1, MOBA_HEADS, MOBA_BLOCK, 128), lambda i: (i, 0, 0, 0)),
            pl.BlockSpec((1, MOBA_HEADS, MOBA_VT_ROWS, MOBA_BLOCK), lambda i: (i, 0, 0, 0)),
            pl.BlockSpec((1, MOBA_HEADS, 128), lambda i: (i, 0, 0)),
        ],
        out_shape=[
            jax.ShapeDtypeStruct((n, MOBA_HEADS * 128), BF16),
            jax.ShapeDtypeStruct((nblk, MOBA_HEADS, MOBA_BLOCK, 128), BF16),
            jax.ShapeDtypeStruct((nblk, MOBA_HEADS, MOBA_VT_ROWS, MOBA_BLOCK), BF16),
            jax.ShapeDtypeStruct((nblk, MOBA_HEADS, 128), F32),
        ],
        compiler_params=_params("parallel"),
        name="moba_prep",
    )(proj, proj, proj, cos_t, sin_t, qg, kg, gmean)


def _moba_attn_kernel(q_ref, k_ref, vt_ref, km_ref, o_ref,
                      qa_ref, sa_ref, sb_ref, sc_ref, sd_ref, ma_ref, mb_ref, mc_ref, md_ref,
                      m_ref, acc_ref):
    i = pl.program_id(1)
    nbat, nb = k_ref.shape[0], k_ref.shape[1]
    B = MOBA_BLOCK
    blk = lax.broadcasted_iota(jnp.int32, (nb, B), 0)
    key_pos = lax.broadcasted_iota(jnp.int32, (B, B), 0)
    qry_pos = lax.broadcasted_iota(jnp.int32, (B, B), 1)
    neg = -jnp.inf
    streams = [(bb, h) for bb in range(nbat) for h in range(2)]

    q_t = [q_ref[bb, :, h * 128:(h + 1) * 128].astype(F32).T for bb, h in streams]
    q_tb = [x.astype(BF16) for x in q_t]

    gates = [_dot(km_ref[h, bb].astype(BF16), q_tb[st]) for st, (bb, h) in enumerate(streams)]
    for st in range(len(streams)):
        g = jnp.where(blk < i, gates[st], neg)
        bias_t = jnp.full((nb, B), MOBA_MASKED, F32)
        for _ in range(MOBA_TOPK):
            mx = jnp.max(g, axis=0, keepdims=True)
            first = jnp.min(jnp.where(g == mx, blk, nb), axis=0, keepdims=True)
            pick = (blk == first) & (mx > neg)
            bias_t = jnp.where(pick, 0.0, bias_t)
            g = jnp.where(pick, neg, g)
        slab_t = jnp.concatenate(
            [jnp.zeros((MOBA_HD, B), F32), bias_t, jnp.zeros((128 - MOBA_HD - nb, B), F32)], axis=0)
        qa_ref[st] = (q_t[st] + slab_t).astype(BF16)
        m_ref[st] = jnp.full((1, B), neg, F32)
        acc_ref[st] = jnp.zeros((MOBA_VT_ROWS, B), F32)

    def stage(st, s, slot):
        s_ref, mx_ref = slot
        s_ref[st] = s
        mx_ref[st] = jnp.max(s, axis=0, keepdims=True)

    def scores_own(slot):
        raw = [_dot(k_ref[bb, i, h], q_tb[st]) for st, (bb, h) in enumerate(streams)]
        for st in range(len(streams)):
            stage(st, jnp.where(key_pos <= qry_pos, raw[st], neg), slot)

    def scores(j, slot):
        raw = [_dot(k_ref[bb, j, h], qa_ref[st]) for st, (bb, h) in enumerate(streams)]
        for st in range(len(streams)):
            stage(st, raw[st], slot)

    def accumulate(j, slot):
        s_ref, mx_ref = slot
        for st, (bb, h) in enumerate(streams):
            m_old = m_ref[st]
            m_new = jnp.maximum(m_old, mx_ref[st])
            alpha = jnp.exp2(m_old - m_new)
            p = jnp.exp2((s_ref[st] - m_new).astype(BF16))
            m_ref[st] = m_new
            acc_ref[st] = alpha * acc_ref[st] + _dot(vt_ref[bb, j, h], p)

    def block_at(n):
        return jnp.minimum(n - 1, i)

    def value_block_at(n):
        return jnp.where(n == 0, i, block_at(n))

    slot_a, slot_b, slot_c, slot_d = ((sa_ref, ma_ref), (sb_ref, mb_ref),
                                      (sc_ref, mc_ref), (sd_ref, md_ref))
    scores_own(slot_a)
    scores(block_at(1), slot_b)

    def body(u, carry):
        n0 = 4 * u
        scores(block_at(n0 + 2), slot_c)
        accumulate(value_block_at(n0), slot_a)
        scores(block_at(n0 + 3), slot_d)
        accumulate(block_at(n0 + 1), slot_b)
        scores(block_at(n0 + 4), slot_a)
        accumulate(block_at(n0 + 2), slot_c)
        scores(block_at(n0 + 5), slot_b)
        accumulate(block_at(n0 + 3), slot_d)
        return carry

    trips = lax.shift_right_logical(i + 2, 2)
    lax.fori_loop(0, trips, body, 0)

    @pl.when(lax.rem(i, 4) < 2)
    def _():
        accumulate(value_block_at(4 * trips), slot_a)
        accumulate(block_at(4 * trips + 1), slot_b)

    for bb in range(nbat):
        out_t = jnp.concatenate(
            [acc_ref[2 * bb + h, 0:MOBA_HD, :] / acc_ref[2 * bb + h, MOBA_HD:MOBA_HD + 1, :]
             for h in range(2)], axis=0)
        o_ref[bb] = out_t.T.astype(o_ref.dtype)


def _moba_attn(qn, kn, vt, km, bsz, t):
    nb = t // MOBA_BLOCK
    npair = MOBA_W // 128
    ns = 2 * bsz
    scores_slot = pltpu.VMEM((ns, MOBA_BLOCK, MOBA_BLOCK), F32)
    row_slot = pltpu.VMEM((ns, 1, MOBA_BLOCK), F32)
    out = pl.pallas_call(
        _moba_attn_kernel,
        grid=(npair, nb),
        in_specs=[
            pl.BlockSpec((bsz, MOBA_BLOCK, 256), lambda p, i: (0, i, p)),
            pl.BlockSpec((bsz, nb, 2, MOBA_BLOCK, 128), lambda p, i: (0, 0, p, 0, 0)),
            pl.BlockSpec((bsz, nb, 2, MOBA_VT_ROWS, MOBA_BLOCK), lambda p, i: (0, 0, p, 0, 0)),
            pl.BlockSpec((2, bsz, nb, 128), lambda p, i: (p, 0, 0, 0)),
        ],
        out_specs=pl.BlockSpec((bsz, MOBA_BLOCK, 128), lambda p, i: (0, i, p)),
        out_shape=jax.ShapeDtypeStruct((bsz, t, MOBA_W), BF16),
        scratch_shapes=[
            pltpu.VMEM((ns, 128, MOBA_BLOCK), BF16),
            scores_slot, scores_slot, scores_slot, scores_slot,
            row_slot, row_slot, row_slot, row_slot,
            row_slot,
            pltpu.VMEM((ns, MOBA_VT_ROWS, MOBA_BLOCK), F32),
        ],
        compiler_params=_params("parallel", "arbitrary"),
        name="moba_attn",
    )(qn.reshape(bsz, t, MOBA_HEADS * 128),
      kn.reshape(bsz, nb, MOBA_HEADS, MOBA_BLOCK, 128),
      vt.reshape(bsz, nb, MOBA_HEADS, MOBA_VT_ROWS, MOBA_BLOCK),
      jnp.transpose(km.reshape(bsz, nb, MOBA_HEADS, 128), (2, 0, 1, 3)))
    return out.reshape(bsz * t, MOBA_W)


def _merge_kernel(x_ref, yg_ref, ym_ref, yl_ref, g0_ref, g1_ref, g2_ref, gb_ref,
                  wg_ref, wm_ref, wl_ref, wo_ref, o_ref):
    gb = gb_ref[...]
    def gate(g_ref, c):
        return _sigmoid(g_ref[...].astype(F32) + gb[:, c * D_MODEL:(c + 1) * D_MODEL])

    mixed = (gate(g0_ref, 0) * _dot(yg_ref[...], wg_ref[...])
             + gate(g1_ref, 1) * _dot(ym_ref[...], wm_ref[...])
             + gate(g2_ref, 2) * _dot(yl_ref[...], wl_ref[...]))
    o_ref[...] = x_ref[...] + _dot(mixed.astype(BF16), wo_ref[...])


def _merge(x, y_gla, y_moba, y_ml, proj, gate_b, w_g, w_m, w_l, w_o, tm):
    n = x.shape[0]
    row = lambda w: pl.BlockSpec((tm, w), lambda i: (i, 0))
    const = lambda shape: pl.BlockSpec(shape, lambda i: (0,) * len(shape))
    gate = lambda c: pl.BlockSpec((tm, D_MODEL), lambda i: (i, COL_GATE0 + c))
    return pl.pallas_call(
        _merge_kernel,
        grid=(n // tm,),
        in_specs=[row(D_MODEL), row(GLA_W), row(MOBA_W), row(MLSTM_V),
                  gate(0), gate(1), gate(2), const((1, N_BRANCH * D_MODEL)),
                  const((GLA_W, D_MODEL)), const((MOBA_W, D_MODEL)), const((MLSTM_V, D_MODEL)),
                  const((D_MODEL, D_MODEL))],
        out_specs=row(D_MODEL),
        out_shape=jax.ShapeDtypeStruct((n, D_MODEL), F32),
        compiler_params=_params("parallel"),
        name="merge",
    )(x, y_gla, y_moba, y_ml, proj, proj, proj, gate_b, w_g, w_m, w_l, w_o)


def _ffn_kernel(x_ref, g_ref, w1_ref, w2_ref, o_ref, h_ref):
    @pl.when(pl.program_id(1) == 0)
    def _():
        x = x_ref[...]
        ms = jnp.mean(x * x, axis=-1, keepdims=True)
        h_ref[...] = (x * lax.rsqrt(ms + EPS) * g_ref[...]).astype(BF16)
        o_ref[...] = x

    a = jnp.square(jnp.maximum(_dot(h_ref[...], w1_ref[...]), 0.0)).astype(BF16)
    o_ref[...] += _dot(a, w2_ref[...])


def _ffn(x, g, w1, w2, tm, tf):
    n = x.shape[0]
    return pl.pallas_call(
        _ffn_kernel,
        grid=(n // tm, D_FF // tf),
        in_specs=[
            pl.BlockSpec((tm, D_MODEL), lambda i, j: (i, 0)),
            pl.BlockSpec((1, D_MODEL), lambda i, j: (0, 0)),
            pl.BlockSpec((D_MODEL, tf), lambda i, j: (0, j)),
            pl.BlockSpec((tf, D_MODEL), lambda i, j: (j, 0)),
        ],
        out_specs=pl.BlockSpec((tm, D_MODEL), lambda i, j: (i, 0)),
        out_shape=jax.ShapeDtypeStruct((n, D_MODEL), F32),
        scratch_shapes=[pltpu.VMEM((tm, D_MODEL), BF16)],
        compiler_params=_params("parallel", "arbitrary"),
        name="ffn",
    )(x, g, w1, w2)


def _rope_tables(seq):
    half = MOBA_HD // 2
    inv = 1.0 / (ROPE_THETA ** (jnp.arange(0, MOBA_HD, 2, dtype=F32) / MOBA_HD))
    ang = jnp.arange(seq, dtype=F32)[:, None] * inv[None, :]
    cos, sin = jnp.cos(ang), jnp.sin(ang)
    cos_t = jnp.tile(cos, (1, 128 // half))
    sin_t = jnp.tile(jnp.concatenate([-sin, sin], axis=1), (1, 128 // MOBA_HD))
    return cos_t, sin_t


def kernel(x, norm1_g, w_in, gla_a_up, gla_a_b, gla_norm_g, moba_qn_g, moba_kn_g, mlstm_conv_w,
           mlstm_i_b, mlstm_f_b, mlstm_norm_g, gate_b, w_br_gla, w_br_moba, w_br_mlstm, w_out,
           norm2_g, w_ff1, w_ff2):
    bsz, t, d = x.shape
    depth = w_in.shape[0]
    n = bsz * t
    assert d == D_MODEL and t % MOBA_BLOCK == 0
    tm = min(512, n)
    tm_big = min(1024, n)
    tb = min(512, t)

    o_ga = 4 * GLA_W
    o_moba = o_ga + GLA_LOWRANK
    o_li = o_moba + 3 * MOBA_W + 2 * MLSTM_QK + 2 * MLSTM_V
    o_gate = o_li + 2 * MLSTM_HEADS
    w_main = jnp.concatenate([w_in[:, :, :o_ga], w_in[:, :, o_moba:o_li], w_in[:, :, o_gate:]],
                             axis=2).astype(BF16)
    w_small = jnp.concatenate(
        [w_in[:, :, o_ga:o_moba], w_in[:, :, o_li:o_gate],
         jnp.zeros((depth, D_MODEL, SMALL_W - GLA_LOWRANK - 2 * MLSTM_HEADS), F32)],
        axis=2).astype(BF16)
    a_up = jnp.concatenate(
        [gla_a_up, jnp.zeros((depth, SMALL_W - GLA_LOWRANK, GLA_W), F32)], axis=1)
    a_up_hi = a_up.astype(BF16)
    a_up = jnp.stack([a_up_hi, (a_up - a_up_hi.astype(F32)).astype(BF16)], axis=1)
    ml_gate_b = jnp.concatenate(
        [jnp.zeros((depth, SM_LI), F32), mlstm_i_b, mlstm_f_b,
         jnp.zeros((depth, SMALL_W - SM_LF - MLSTM_HEADS), F32)], axis=1)
    qn_g = jnp.tile(moba_qn_g, (1, 128 // MOBA_HD))
    kn_g = jnp.tile(moba_kn_g, (1, 128 // MOBA_HD))
    w_g, w_m, w_l, w_o = (w.astype(BF16) for w in (w_br_gla, w_br_moba, w_br_mlstm, w_out))
    w1, w2 = w_ff1.astype(BF16), w_ff2.astype(BF16)

    cos_t, sin_t = _rope_tables(t)
    grp = jnp.arange(128) // MOBA_HD
    gmean = ((grp[:, None] == grp[None, :]).astype(F32) / MOBA_HD).astype(BF16)
    tril = jnp.tril(jnp.ones((GLA_CHUNK, GLA_CHUNK), BF16))

    xf = x.reshape(n, d)
    for l in range(depth):
        proj, small = _inproj(xf, norm1_g[l][None], w_main[l], w_small[l], tm_big, 2048)
        y_gla, y_ml = _recurrent(proj, small, a_up[l], gla_a_b[l][None], gla_norm_g[l][None],
                                 mlstm_conv_w[l], ml_gate_b[l][None], mlstm_norm_g[l][None],
                                 tril, bsz, t, tb)
        qn, kn, vt, km = _moba_prep(proj, cos_t, sin_t, qn_g[l][None], kn_g[l][None], gmean, t)
        y_moba = _moba_attn(qn, kn, vt, km, bsz, t)
        xf = _merge(xf, y_gla, y_moba, y_ml, proj, gate_b[l][None], w_g[l], w_m[l], w_l[l],
                    w_o[l], tm)
        xf = _ffn(xf, norm2_g[l][None], w1[l], w2[l], tm_big, 1024)
    return xf.reshape(bsz, t, d)
```

```python
import functools

import jax
import jax.numpy as jnp
from jax import lax
from jax.experimental import pallas as pl
from jax.experimental.pallas import tpu as pltpu

F32 = jnp.float32
BF16 = jnp.bfloat16

D_MODEL = 1024
GLA_HEADS, GLA_DK, GLA_DV, GLA_LOWRANK = 4, 128, 128, 16
GLA_GATE_NORM = 16.0
GLA_CHUNK = 256
GLA_SUB = 64
MOBA_HEADS, MOBA_HD, MOBA_BLOCK, MOBA_TOPK = 8, 64, 256, 3
ROPE_THETA = 10000.0
MLSTM_HEADS, MLSTM_DQK, MLSTM_DV = 4, 64, 128
MLSTM_CHUNK = 256
MLSTM_CONV = 4
D_FF = 4 * D_MODEL
N_BRANCH = 3
EPS = 1e-6

GLA_W = GLA_HEADS * GLA_DK
MOBA_W = MOBA_HEADS * MOBA_HD
MLSTM_QK = MLSTM_HEADS * MLSTM_DQK
MLSTM_V = MLSTM_HEADS * MLSTM_DV
MOBA_QSCALE = MOBA_HD ** -0.5 * 1.4426950408889634
MOBA_MASKED = -1e30
MOBA_VT_ROWS = MOBA_HD + 16

COL_GLA_Q, COL_GLA_K, COL_GLA_V, COL_GLA_G = 0, 1, 2, 3
COL_MOBA_Q, COL_MOBA_K, COL_MOBA_V = 4, 5, 6
COL_ML_QK, COL_ML_V, COL_ML_O = 7, 8, 9
COL_GATE0 = 5
N_MAIN = 8192
SM_GA, SM_LI, SM_LF = 0, 16, 20
SMALL_W = 128

FFN_SUB = 1024
VMEM_LIMIT = 48 * 1024 * 1024


def _dot(a, b, precision=None):
    return jnp.dot(a, b, preferred_element_type=F32, precision=precision)


def _dot_nt(a, b):
    return lax.dot_general(a, b, (((1,), (1,)), ((), ())), preferred_element_type=F32)


def _dot_tn(a, b):
    return lax.dot_general(a, b, (((0,), (0,)), ((), ())), preferred_element_type=F32)


def _split2(x):
    hi = x.astype(BF16)
    lo = (x - hi.astype(F32)).astype(BF16)
    return hi, lo


def _cumsum_rows(tril_bf16, x):
    hi, lo = _split2(x)
    return _dot(tril_bf16, hi) + _dot(tril_bf16, lo)


def _dot_exact_rhs(x, w_bf16):
    hi, lo = _split2(x)
    return _dot(hi, w_bf16) + _dot(lo, w_bf16)


def _sigmoid(x):
    return 1.0 / (1.0 + jnp.exp(-x))


def _log_sigmoid(x):
    return jnp.minimum(x, 0.0) - jnp.log(1.0 + jnp.exp(-jnp.abs(x)))


def _params(*sem):
    return pltpu.CompilerParams(dimension_semantics=sem, vmem_limit_bytes=VMEM_LIMIT)


def _inproj_kernel(x_ref, g_ref, w_ref, ws_ref, out_ref, small_ref, hn_ref):
    @pl.when(pl.program_id(1) == 0)
    def _():
        x = x_ref[...]
        ms = jnp.mean(x * x, axis=-1, keepdims=True)
        hn = (x * lax.rsqrt(ms + EPS) * g_ref[...]).astype(BF16)
        hn_ref[...] = hn
        small_ref[...] = _dot(hn, ws_ref[...])

    out_ref[...] = _dot(hn_ref[...], w_ref[...]).astype(out_ref.dtype)


def _inproj(x, g, w_main, w_small, tm, tn):
    n = x.shape[0]
    return pl.pallas_call(
        _inproj_kernel,
        grid=(n // tm, N_MAIN // tn),
        in_specs=[
            pl.BlockSpec((tm, D_MODEL), lambda i, j: (i, 0)),
            pl.BlockSpec((1, D_MODEL), lambda i, j: (0, 0)),
            pl.BlockSpec((D_MODEL, tn), lambda i, j: (0, j)),
            pl.BlockSpec((D_MODEL, SMALL_W), lambda i, j: (0, 0)),
        ],
        out_specs=[
            pl.BlockSpec((tm, tn), lambda i, j: (i, j)),
            pl.BlockSpec((tm, SMALL_W), lambda i, j: (i, 0)),
        ],
        out_shape=[
            jax.ShapeDtypeStruct((n, N_MAIN), BF16),
            jax.ShapeDtypeStruct((n, SMALL_W), F32),
        ],
        scratch_shapes=[pltpu.VMEM((tm, D_MODEL), BF16)],
        compiler_params=_params("parallel", "arbitrary"),
        name="inproj",
    )(x, g, w_main, w_small)


def _gla_setup(q_ref, k_ref, v_ref, g_ref, small_ref, aup_ref, ab_ref, ng_ref, tril_ref,
               o_ref, st_ref, la_ref):
    L = GLA_CHUNK

    @pl.when(pl.program_id(1) == 0)
    def _():
        st_ref[...] = jnp.zeros_like(st_ref)

    z = (_dot_exact_rhs(small_ref[...], aup_ref[0])
         + _dot(small_ref[...].astype(BF16), aup_ref[1]) + ab_ref[...])
    la_ref[...] = _log_sigmoid(z) * (1.0 / GLA_GATE_NORM)

    tril = tril_ref[...]
    S = GLA_SUB
    nsub = L // S
    scale = GLA_DK ** -0.5
    ng = ng_ref[...]

    heads = range(GLA_HEADS)
    hq = [slice(h * GLA_DK, (h + 1) * GLA_DK) for h in heads]
    hv = [slice(h * GLA_DV, (h + 1) * GLA_DV) for h in heads]
    sub = [slice(a * S, (a + 1) * S) for a in range(nsub)]

    def chunk(rows):
        bcum = _cumsum_rows(tril, la_ref[rows, :])
        yield
        b_last = bcum[L - 1:L, :]
        q = q_ref[rows, :].astype(F32) * scale
        k = k_ref[rows, :].astype(F32)
        v = v_ref[rows, :]
        c = [jnp.zeros((1, GLA_W), F32)] + [bcum[a * S - 1:a * S, :] for a in range(1, nsub)]
        c_rows = jnp.concatenate([jnp.broadcast_to(c[a], (S, GLA_W)) for a in range(nsub)], axis=0)
        e_loc = jnp.exp(bcum - c_rows)
        q_loc = q * e_loc
        k_loc32 = k * (1.0 / e_loc)
        k_loc = k_loc32.astype(BF16)
        q_dec = jnp.concatenate([q_loc[sub[a], :] * jnp.exp(c[a]) for a in range(nsub)],
                                axis=0).astype(BF16)
        k_end = jnp.concatenate([k_loc32[sub[a], :] * jnp.exp(b_last - c[a]) for a in range(nsub)],
                                axis=0).astype(BF16)
        decay = jnp.exp(b_last)
        q_vs = [jnp.concatenate(
            [(q_loc[sub[a], :] if a == b else q_loc[sub[a], :] * jnp.exp(c[a] - c[b])).astype(BF16)
             for a in range(b, nsub)], axis=0) for b in range(nsub)]
        st = [st_ref[h] for h in heads]
        yield
        attn = [[_dot_nt(q_vs[b][:, hq[h]], k_loc[sub[b], hq[h]]) for h in heads]
                for b in range(nsub)]
        o_inter = [_dot_nt(q_dec[:, hq[h]], st[h].astype(BF16)) for h in heads]
        kv = [_dot_tn(v[:, hv[h]], k_end[:, hq[h]]) for h in heads]
        yield
        pv = []
        for b in range(nsub):
            shape = (L - b * S, S)
            keep = (lax.broadcasted_iota(jnp.int32, shape, 0)
                    >= lax.broadcasted_iota(jnp.int32, shape, 1))
            pv.append([_dot(jnp.where(keep, attn[b][h], 0.0).astype(BF16), v[sub[b], hv[h]])
                       for h in heads])
        yield
        for h in heads:
            o = jnp.concatenate(
                [o_inter[h][sub[a], :] + sum(pv[b][h][(a - b) * S:(a - b + 1) * S, :]
                                             for b in range(a + 1)) for a in range(nsub)], axis=0)
            st_ref[h] = st[h] * decay[:, hq[h]] + kv[h]
            ms = jnp.mean(o * o, axis=-1, keepdims=True)
            y = o * lax.rsqrt(ms + EPS) * ng
            gg = g_ref[rows, hv[h]].astype(F32)
            o_ref[rows, hv[h]] = (y * (gg * _sigmoid(gg))).astype(o_ref.dtype)

    return chunk


def _mlstm_setup(qk_ref, v_ref, lo_ref, small_ref, cw_ref, gb_ref, ng_ref, tril_ref,
                 out_ref, ext_ref, qkc_ref, gc_ref, st_ref, m_ref):
    tb = qk_ref.shape[0]
    L = MLSTM_CHUNK
    pad = 8

    @pl.when(pl.program_id(1) == 0)
    def _():
        st_ref[...] = jnp.zeros_like(st_ref)
        m_ref[...] = jnp.zeros_like(m_ref)
        ext_ref[0:pad, :] = jnp.zeros((pad, ext_ref.shape[1]), F32)

    x = qk_ref[...].astype(F32)
    ext_ref[pad:pad + tb, :] = x
    acc = None
    for w in range(MLSTM_CONV):
        off = pad - (MLSTM_CONV - 1) + w
        term = ext_ref[off:off + tb, :] * cw_ref[w:w + 1, :]
        acc = term if acc is None else acc + term
    qkc_ref[...] = acc
    ext_ref[0:pad, :] = x[tb - pad:tb, :]

    gsm = small_ref[...] + gb_ref[...]
    lane = lax.broadcasted_iota(jnp.int32, gsm.shape, 1)
    is_f = (lane >= SM_LF) & (lane < SM_LF + MLSTM_HEADS)
    gc_ref[...] = jnp.where(is_f, _log_sigmoid(gsm), gsm)

    tril = tril_ref[...]
    causal_t = (lax.broadcasted_iota(jnp.int32, (L, L), 0)
                <= lax.broadcasted_iota(jnp.int32, (L, L), 1))
    lane_c = lax.broadcasted_iota(jnp.int32, (L, SMALL_W), 1)
    is_f_c = (lane_c >= SM_LF) & (lane_c < SM_LF + MLSTM_HEADS)
    ones_rows = jnp.ones((MLSTM_DV, L), F32)
    ng = ng_ref[...]
    kscale = MLSTM_DQK ** -0.5
    neg = -jnp.inf

    def chunk(rows):
        gates = gc_ref[rows, :]
        cum = _cumsum_rows(tril, gates)
        yield
        pc = jnp.where(is_f_c, cum, gates)
        pr = pc.T
        pcu = pc - pltpu.roll(pc, SMALL_W - (SM_LF - SM_LI), 1)
        heads = range(MLSTM_HEADS)
        hv = [slice(h * MLSTM_DV, (h + 1) * MLSTM_DV) for h in heads]
        qk_all = qkc_ref[rows, :]
        st = [st_ref[h] for h in heads]

        d_t, inter_w, m_row, w_r, m_new, a_scale, c_scale = [], [], [], [], [], [], []
        for h in heads:
            b_r = pr[SM_LF + h:SM_LF + h + 1, :]
            u_r = pr[SM_LI + h:SM_LI + h + 1, :] - b_r
            u_c = pcu[:, SM_LI + h:SM_LI + h + 1]
            b_last = b_r[:, L - 1:L]
            m_prev = m_ref[h:h + 1, 0:1]
            um = jnp.where(causal_t, u_c, neg)
            run_max = jnp.max(um, axis=0, keepdims=True)
            m_row.append(b_r + jnp.maximum(m_prev, run_max))
            d_t.append(jnp.exp(um + (b_r - m_row[h])))
            inter_w.append(jnp.exp(b_r + m_prev - m_row[h]))
            m_loc = jnp.max(u_r, axis=-1, keepdims=True) + b_last
            w_r.append(jnp.exp(u_r + b_last - m_loc))
            m_new.append(jnp.maximum(b_last + m_prev, m_loc))
            a_scale.append(jnp.exp(b_last + m_prev - m_new[h]))
            c_scale.append(jnp.exp(m_loc - m_new[h]))

        yield
        q = [qk_all[:, h * MLSTM_DQK:(h + 1) * MLSTM_DQK].astype(BF16) for h in heads]
        k = [(qk_all[:, MLSTM_QK + h * MLSTM_DQK:MLSTM_QK + (h + 1) * MLSTM_DQK]
              * kscale).astype(BF16) for h in heads]
        v_aug_t = [jnp.concatenate([v_ref[rows, hv[h]].astype(F32).T, ones_rows], axis=0)
                   for h in heads]
        s_t = [_dot_nt(k[h], q[h]) for h in heads]
        inter_t = [_dot_nt(st[h].astype(BF16), q[h]) for h in heads]
        kv_t = [_dot((v_aug_t[h] * w_r[h]).astype(BF16), k[h]) for h in heads]

        yield
        sw = [(s_t[h] * d_t[h]).astype(BF16) for h in heads]
        intra_t = [_dot(v_aug_t[h].astype(BF16), sw[h]) for h in heads]
        yield
        for h in heads:
            tot = intra_t[h] + inter_w[h] * inter_t[h]
            num = tot[:MLSTM_DV, :]
            den = tot[MLSTM_DV:, :]
            hid_t = num / jnp.maximum(jnp.abs(den), jnp.exp(-m_row[h]))
            st_ref[h] = a_scale[h] * st[h] + c_scale[h] * kv_t[h]
            m_ref[h:h + 1, :] = jnp.broadcast_to(m_new[h], (1, m_ref.shape[1]))
            ms = jnp.mean(hid_t * hid_t, axis=0, keepdims=True)
            y = (hid_t * lax.rsqrt(ms + EPS)).T * ng
            out_ref[rows, hv[h]] = (y * _sigmoid(lo_ref[rows, hv[h]].astype(F32))).astype(out_ref.dtype)

    return chunk


def _interleave(*gens):
    live = list(gens)
    while live:
        live = [g for g in live if next(g, True) is None]


def _recurrent_kernel(gq_ref, gk_ref, gv_ref, gg_ref, lqk_ref, lv_ref, lo_ref, small_ref,
                      aup_ref, ab_ref, gng_ref, cw_ref, gb_ref, lng_ref, tril_ref,
                      yg_ref, yl_ref,
                      gst_ref, la_ref, ext_ref, qkc_ref, gc_ref, lst_ref, m_ref):
    assert GLA_CHUNK == MLSTM_CHUNK
    L = GLA_CHUNK
    tb = gq_ref.shape[0]
    gla_chunk = _gla_setup(gq_ref, gk_ref, gv_ref, gg_ref, small_ref, aup_ref, ab_ref, gng_ref,
                           tril_ref, yg_ref, gst_ref, la_ref)
    mlstm_chunk = _mlstm_setup(lqk_ref, lv_ref, lo_ref, small_ref, cw_ref, gb_ref, lng_ref,
                               tril_ref, yl_ref, ext_ref, qkc_ref, gc_ref, lst_ref, m_ref)

    def body(c, carry):
        rows = pl.ds(pl.multiple_of(c * L, L), L)
        _interleave(gla_chunk(rows), mlstm_chunk(rows))
        return carry

    lax.fori_loop(0, tb // L, body, 0)


def _recurrent(proj, small, a_up, a_b, gla_ng, conv_w, gate_bias, ml_ng, tril, bsz, t, tb):
    n = proj.shape[0]
    nt = t // tb
    width = GLA_W
    assert MLSTM_V == width and 2 * MLSTM_QK == width

    def col(cidx):
        return pl.BlockSpec((tb, width), lambda b, s: (b * nt + s, cidx))

    const = lambda shape: pl.BlockSpec(shape, lambda b, s: (0,) * len(shape))
    out = pl.BlockSpec((tb, width), lambda b, s: (b * nt + s, 0))
    return pl.pallas_call(
        _recurrent_kernel,
        grid=(bsz, nt),
        in_specs=[
            col(COL_GLA_Q), col(COL_GLA_K), col(COL_GLA_V), col(COL_GLA_G),
            col(COL_ML_QK), col(COL_ML_V), col(COL_ML_O),
            pl.BlockSpec((tb, SMALL_W), lambda b, s: (b * nt + s, 0)),
            const((2, SMALL_W, GLA_W)), const((1, GLA_W)), const((1, GLA_DV)),
            const((MLSTM_CONV, 2 * MLSTM_QK)), const((1, SMALL_W)), const((1, MLSTM_DV)),
            const((GLA_CHUNK, GLA_CHUNK)),
        ],
        out_specs=[out, out],
        out_shape=[jax.ShapeDtypeStruct((n, width), BF16), jax.ShapeDtypeStruct((n, width), BF16)],
        scratch_shapes=[
            pltpu.VMEM((GLA_HEADS, GLA_DV, GLA_DK), F32),
            pltpu.VMEM((tb, GLA_W), F32),
            pltpu.VMEM((tb + 8, 2 * MLSTM_QK), F32),
            pltpu.VMEM((tb, 2 * MLSTM_QK), F32),
            pltpu.VMEM((tb, SMALL_W), F32),
            pltpu.VMEM((MLSTM_HEADS, 2 * MLSTM_DV, MLSTM_DQK), F32),
            pltpu.VMEM((8, 128), F32),
        ],
        compiler_params=_params("parallel", "arbitrary"),
        name="recurrent",
    )(proj, proj, proj, proj, proj, proj, proj, small, a_up, a_b, gla_ng, conv_w, gate_bias,
      ml_ng, tril)


def _moba_prep_kernel(q_ref, k_ref, v_ref, cos_ref, sin_ref, qg_ref, kg_ref, gm_ref,
                      qo_ref, ko_ref, vt_ref, km_ref, *, tblk):
    gm = gm_ref[...]
    B = MOBA_BLOCK
    lane = lax.broadcasted_iota(jnp.int32, (B, 128), 1)
    first_half = (lane % MOBA_HD) < (MOBA_HD // 2)
    low = lane < MOBA_HD

    for sb in range(ko_ref.shape[0]):
        rs = slice(sb * B, (sb + 1) * B)
        cos = cos_ref[rs, :]
        sin = sin_ref[rs, :]
        j = (pl.program_id(0) * ko_ref.shape[0] + sb) % tblk
        onehot = jnp.where(lane == MOBA_HD + j, 1.0, 0.0)

        def norm_rope(x, g):
            ms = _dot_exact_rhs(x * x, gm)
            y = x * lax.rsqrt(ms + EPS) * g
            partner = jnp.where(first_half, pltpu.roll(y, 128 - MOBA_HD // 2, 1),
                                pltpu.roll(y, MOBA_HD // 2, 1))
            return y * cos + partner * sin

        for p in range(MOBA_W // 128):
            cs = slice(p * 128, (p + 1) * 128)
            qn = norm_rope(q_ref[rs, cs].astype(F32), qg_ref[...]) * MOBA_QSCALE
            kn = norm_rope(k_ref[rs, cs].astype(F32), kg_ref[...])
            v_t = v_ref[rs, cs].astype(F32).T.astype(BF16)
            for h in range(2):
                hh = 2 * p + h
                qh = qn if h == 0 else pltpu.roll(qn, MOBA_HD, 1)
                kh = kn if h == 0 else pltpu.roll(kn, MOBA_HD, 1)
                qo_ref[rs, hh * 128:(hh + 1) * 128] = jnp.where(low, qh, 0.0).astype(BF16)
                ka = jnp.where(low, kh, onehot)
                ko_ref[sb, hh] = ka.astype(BF16)
                km_ref[sb, hh:hh + 1, :] = jnp.mean(ka, axis=0, keepdims=True)
                vt_ref[sb, hh, 0:MOBA_HD, :] = v_t[h * MOBA_HD:(h + 1) * MOBA_HD, :]
                vt_ref[sb, hh, MOBA_HD:MOBA_VT_ROWS, :] = jnp.ones(
                    (MOBA_VT_ROWS - MOBA_HD, MOBA_BLOCK), BF16)


def _moba_prep(proj, cos_t, sin_t, qg, kg, gmean, t):
    n = proj.shape[0]
    nblk = n // MOBA_BLOCK
    tblk = t // MOBA_BLOCK

    per = 2 if tblk % 2 == 0 else 1
    rows = per * MOBA_BLOCK
    tsteps = tblk // per

    def col(cidx):
        return pl.BlockSpec((rows, MOBA_W), lambda i: (i, cidx))

    const = lambda shape: pl.BlockSpec(shape, lambda i: (0,) * len(shape))
    tab = pl.BlockSpec((rows, 128), lambda i: (i % tsteps, 0))
    return pl.pallas_call(
        functools.partial(_moba_prep_kernel, tblk=tblk),
        grid=(nblk // per,),
        in_specs=[col(COL_MOBA_Q), col(COL_MOBA_K), col(COL_MOBA_V), tab, tab,
                  const((1, 128)), const((1, 128)), const((128, 128))],
        out_specs=[
            pl.BlockSpec((rows, MOBA_HEADS * 128), lambda i: (i, 0)),
            pl.BlockSpec((per, MOBA_HEADS, MOBA_BLOCK, 128), lambda i: (i, 0, 0, 0)),
            pl.BlockSpec((per, MOBA_HEADS, MOBA_VT_ROWS, MOBA_BLOCK), lambda i: (i, 0, 0, 0)),
            pl.BlockSpec((per, MOBA_HEADS, 128), lambda i: (i, 0, 0)),
        ],
        out_shape=[
            jax.ShapeDtypeStruct((n, MOBA_HEADS * 128), BF16),
            jax.ShapeDtypeStruct((nblk, MOBA_HEADS, MOBA_BLOCK, 128), BF16),
            jax.ShapeDtypeStruct((nblk, MOBA_HEADS, MOBA_VT_ROWS, MOBA_BLOCK), BF16),
            jax.ShapeDtypeStruct((nblk, MOBA_HEADS, 128), F32),
        ],
        compiler_params=_params("parallel"),
        name="moba_prep",
    )(proj, proj, proj, cos_t, sin_t, qg, kg, gmean)


def _moba_attn_kernel(q_ref, k_ref, vt_ref, km_ref, o_ref,
                      qa_ref, sa_ref, sb_ref, sc_ref, sd_ref, ma_ref, mb_ref, mc_ref, md_ref,
                      m_ref, acc_ref):
    i = pl.program_id(1)
    nbat, nb = k_ref.shape[0], k_ref.shape[1]
    B = MOBA_BLOCK
    blk = lax.broadcasted_iota(jnp.int32, (nb, B), 0)
    key_pos = lax.broadcasted_iota(jnp.int32, (B, B), 0)
    qry_pos = lax.broadcasted_iota(jnp.int32, (B, B), 1)
    neg = -jnp.inf
    streams = [(bb, h) for bb in range(nbat) for h in range(2)]

    q_t = [q_ref[bb, :, h * 128:(h + 1) * 128].astype(F32).T for bb, h in streams]
    q_tb = [x.astype(BF16) for x in q_t]

    gates = [_dot(km_ref[h, bb].astype(BF16), q_tb[st]) for st, (bb, h) in enumerate(streams)]
    for st in range(len(streams)):
        g = jnp.where(blk < i, gates[st], neg)
        bias_t = jnp.full((nb, B), MOBA_MASKED, F32)
        for _ in range(MOBA_TOPK):
            mx = jnp.max(g, axis=0, keepdims=True)
            first = jnp.min(jnp.where(g == mx, blk, nb), axis=0, keepdims=True)
            pick = (blk == first) & (mx > neg)
            bias_t = jnp.where(pick, 0.0, bias_t)
            g = jnp.where(pick, neg, g)
        slab_t = jnp.concatenate(
            [jnp.zeros((MOBA_HD, B), F32), bias_t, jnp.zeros((128 - MOBA_HD - nb, B), F32)], axis=0)
        qa_ref[st] = (q_t[st] + slab_t).astype(BF16)
        m_ref[st] = jnp.full((1, B), neg, F32)
        acc_ref[st] = jnp.zeros((MOBA_VT_ROWS, B), F32)

    def stage(st, s, slot):
        s_ref, mx_ref = slot
        s_ref[st] = s
        mx_ref[st] = jnp.max(s, axis=0, keepdims=True)

    def scores_own(slot):
        raw = [_dot(k_ref[bb, i, h], q_tb[st]) for st, (bb, h) in enumerate(streams)]
        for st in range(len(streams)):
            stage(st, jnp.where(key_pos <= qry_pos, raw[st], neg), slot)

    def scores(j, slot):
        raw = [_dot(k_ref[bb, j, h], qa_ref[st]) for st, (bb, h) in enumerate(streams)]
        for st in range(len(streams)):
            stage(st, raw[st], slot)

    def accumulate(j, slot):
        s_ref, mx_ref = slot
        for st, (bb, h) in enumerate(streams):
            m_old = m_ref[st]
            m_new = jnp.maximum(m_old, mx_ref[st])
            alpha = jnp.exp2(m_old - m_new)
            p = jnp.exp2((s_ref[st] - m_new).astype(BF16))
            m_ref[st] = m_new
            acc_ref[st] = alpha * acc_ref[st] + _dot(vt_ref[bb, j, h], p)

    def block_at(n):
        return jnp.minimum(n - 1, i)

    def value_block_at(n):
        return jnp.where(n == 0, i, block_at(n))

    slot_a, slot_b, slot_c, slot_d = ((sa_ref, ma_ref), (sb_ref, mb_ref),
                                      (sc_ref, mc_ref), (sd_ref, md_ref))
    scores_own(slot_a)
    scores(block_at(1), slot_b)

    def body(u, carry):
        n0 = 4 * u
        scores(block_at(n0 + 2), slot_c)
        accumulate(value_block_at(n0), slot_a)
        scores(block_at(n0 + 3), slot_d)
        accumulate(block_at(n0 + 1), slot_b)
        scores(block_at(n0 + 4), slot_a)
        accumulate(block_at(n0 + 2), slot_c)
        scores(block_at(n0 + 5), slot_b)
        accumulate(block_at(n0 + 3), slot_d)
        return carry

    trips = lax.shift_right_logical(i + 2, 2)
    lax.fori_loop(0, trips, body, 0)

    @pl.when(lax.rem(i, 4) < 2)
    def _():
        accumulate(value_block_at(4 * trips), slot_a)
        accumulate(block_at(4 * trips + 1), slot_b)

    for bb in range(nbat):
        out_t = jnp.concatenate(
            [acc_ref[2 * bb + h, 0:MOBA_HD, :] / acc_ref[2 * bb + h, MOBA_HD:MOBA_HD + 1, :]
             for h in range(2)], axis=0)
        o_ref[bb] = out_t.T.astype(o_ref.dtype)


def _moba_attn(qn, kn, vt, km, bsz, t):
    nb = t // MOBA_BLOCK
    npair = MOBA_W // 128
    ns = 2 * bsz
    scores_slot = pltpu.VMEM((ns, MOBA_BLOCK, MOBA_BLOCK), F32)
    row_slot = pltpu.VMEM((ns, 1, MOBA_BLOCK), F32)
    out = pl.pallas_call(
        _moba_attn_kernel,
        grid=(npair, nb),
        in_specs=[
            pl.BlockSpec((bsz, MOBA_BLOCK, 256), lambda p, i: (0, i, p)),
            pl.BlockSpec((bsz, nb, 2, MOBA_BLOCK, 128), lambda p, i: (0, 0, p, 0, 0)),
            pl.BlockSpec((bsz, nb, 2, MOBA_VT_ROWS, MOBA_BLOCK), lambda p, i: (0, 0, p, 0, 0)),
            pl.BlockSpec((2, bsz, nb, 128), lambda p, i: (p, 0, 0, 0)),
        ],
        out_specs=pl.BlockSpec((bsz, MOBA_BLOCK, 128), lambda p, i: (0, i, p)),
        out_shape=jax.ShapeDtypeStruct((bsz, t, MOBA_W), BF16),
        scratch_shapes=[
            pltpu.VMEM((ns, 128, MOBA_BLOCK), BF16),
            scores_slot, scores_slot, scores_slot, scores_slot,
            row_slot, row_slot, row_slot, row_slot,
            row_slot,
            pltpu.VMEM((ns, MOBA_VT_ROWS, MOBA_BLOCK), F32),
        ],
        compiler_params=_params("parallel", "arbitrary"),
        name="moba_attn",
    )(qn.reshape(bsz, t, MOBA_HEADS * 128),
      kn.reshape(bsz, nb, MOBA_HEADS, MOBA_BLOCK, 128),
      vt.reshape(bsz, nb, MOBA_HEADS, MOBA_VT_ROWS, MOBA_BLOCK),
      jnp.transpose(km.reshape(bsz, nb, MOBA_HEADS, 128), (2, 0, 1, 3)))
    return out.reshape(bsz * t, MOBA_W)


def _merge_kernel(x_ref, yg_ref, ym_ref, yl_ref, g0_ref, g1_ref, g2_ref, gb_ref,
                  wg_ref, wm_ref, wl_ref, wo_ref, o_ref):
    gb = gb_ref[...]
    def gate(g_ref, c):
        return _sigmoid(g_ref[...].astype(F32) + gb[:, c * D_MODEL:(c + 1) * D_MODEL])

    mixed = (gate(g0_ref, 0) * _dot(yg_ref[...], wg_ref[...])
             + gate(g1_ref, 1) * _dot(ym_ref[...], wm_ref[...])
             + gate(g2_ref, 2) * _dot(yl_ref[...], wl_ref[...]))
    o_ref[...] = x_ref[...] + _dot(mixed.astype(BF16), wo_ref[...])


def _merge(x, y_gla, y_moba, y_ml, proj, gate_b, w_g, w_m, w_l, w_o, tm):
    n = x.shape[0]
    row = lambda w: pl.BlockSpec((tm, w), lambda i: (i, 0))
    const = lambda shape: pl.BlockSpec(shape, lambda i: (0,) * len(shape))
    gate = lambda c: pl.BlockSpec((tm, D_MODEL), lambda i: (i, COL_GATE0 + c))
    return pl.pallas_call(
        _merge_kernel,
        grid=(n // tm,),
        in_specs=[row(D_MODEL), row(GLA_W), row(MOBA_W), row(MLSTM_V),
                  gate(0), gate(1), gate(2), const((1, N_BRANCH * D_MODEL)),
                  const((GLA_W, D_MODEL)), const((MOBA_W, D_MODEL)), const((MLSTM_V, D_MODEL)),
                  const((D_MODEL, D_MODEL))],
        out_specs=row(D_MODEL),
        out_shape=jax.ShapeDtypeStruct((n, D_MODEL), F32),
        compiler_params=_params("parallel"),
        name="merge",
    )(x, y_gla, y_moba, y_ml, proj, proj, proj, gate_b, w_g, w_m, w_l, w_o)


def _ffn_kernel(x_ref, g_ref, w1_ref, w2_ref, o_ref, h_ref):
    @pl.when(pl.program_id(1) == 0)
    def _():
        x = x_ref[...]
        ms = jnp.mean(x * x, axis=-1, keepdims=True)
        h_ref[...] = (x * lax.rsqrt(ms + EPS) * g_ref[...]).astype(BF16)
        o_ref[...] = x

    for c in range(w1_ref.shape[1] // FFN_SUB):
        cols = slice(c * FFN_SUB, (c + 1) * FFN_SUB)
        a = jnp.square(jnp.maximum(_dot(h_ref[...], w1_ref[:, cols]), 0.0)).astype(BF16)
        o_ref[...] += _dot(a, w2_ref[cols, :])


def _ffn(x, g, w1, w2, tm, tf):
    n = x.shape[0]
    return pl.pallas_call(
        _ffn_kernel,
        grid=(n // tm, D_FF // tf),
        in_specs=[
            pl.BlockSpec((tm, D_MODEL), lambda i, j: (i, 0)),
            pl.BlockSpec((1, D_MODEL), lambda i, j: (0, 0)),
            pl.BlockSpec((D_MODEL, tf), lambda i, j: (0, j)),
            pl.BlockSpec((tf, D_MODEL), lambda i, j: (j, 0)),
        ],
        out_specs=pl.BlockSpec((tm, D_MODEL), lambda i, j: (i, 0)),
        out_shape=jax.ShapeDtypeStruct((n, D_MODEL), F32),
        scratch_shapes=[pltpu.VMEM((tm, D_MODEL), BF16)],
        compiler_params=_params("parallel", "arbitrary"),
        name="ffn",
    )(x, g, w1, w2)


def _rope_tables(seq):
    half = MOBA_HD // 2
    inv = 1.0 / (ROPE_THETA ** (jnp.arange(0, MOBA_HD, 2, dtype=F32) / MOBA_HD))
    ang = jnp.arange(seq, dtype=F32)[:, None] * inv[None, :]
    cos, sin = jnp.cos(ang), jnp.sin(ang)
    cos_t = jnp.tile(cos, (1, 128 // half))
    sin_t = jnp.tile(jnp.concatenate([-sin, sin], axis=1), (1, 128 // MOBA_HD))
    return cos_t, sin_t


def kernel(x, norm1_g, w_in, gla_a_up, gla_a_b, gla_norm_g, moba_qn_g, moba_kn_g, mlstm_conv_w,
           mlstm_i_b, mlstm_f_b, mlstm_norm_g, gate_b, w_br_gla, w_br_moba, w_br_mlstm, w_out,
           norm2_g, w_ff1, w_ff2):
    bsz, t, d = x.shape
    depth = w_in.shape[0]
    n = bsz * t
    assert d == D_MODEL and t % MOBA_BLOCK == 0
    tm = min(512, n)
    tm_big = min(1024, n)
    tb = min(512, t)

    o_ga = 4 * GLA_W
    o_moba = o_ga + GLA_LOWRANK
    o_li = o_moba + 3 * MOBA_W + 2 * MLSTM_QK + 2 * MLSTM_V
    o_gate = o_li + 2 * MLSTM_HEADS
    w_main = jnp.concatenate([w_in[:, :, :o_ga], w_in[:, :, o_moba:o_li], w_in[:, :, o_gate:]],
                             axis=2).astype(BF16)
    w_small = jnp.concatenate(
        [w_in[:, :, o_ga:o_moba], w_in[:, :, o_li:o_gate],
         jnp.zeros((depth, D_MODEL, SMALL_W - GLA_LOWRANK - 2 * MLSTM_HEADS), F32)],
        axis=2).astype(BF16)
    a_up = jnp.concatenate(
        [gla_a_up, jnp.zeros((depth, SMALL_W - GLA_LOWRANK, GLA_W), F32)], axis=1)
    a_up_hi = a_up.astype(BF16)
    a_up = jnp.stack([a_up_hi, (a_up - a_up_hi.astype(F32)).astype(BF16)], axis=1)
    ml_gate_b = jnp.concatenate(
        [jnp.zeros((depth, SM_LI), F32), mlstm_i_b, mlstm_f_b,
         jnp.zeros((depth, SMALL_W - SM_LF - MLSTM_HEADS), F32)], axis=1)
    qn_g = jnp.tile(moba_qn_g, (1, 128 // MOBA_HD))
    kn_g = jnp.tile(moba_kn_g, (1, 128 // MOBA_HD))
    w_g, w_m, w_l, w_o = (w.astype(BF16) for w in (w_br_gla, w_br_moba, w_br_mlstm, w_out))
    w1, w2 = w_ff1.astype(BF16), w_ff2.astype(BF16)

    cos_t, sin_t = _rope_tables(t)
    grp = jnp.arange(128) // MOBA_HD
    gmean = ((grp[:, None] == grp[None, :]).astype(F32) / MOBA_HD).astype(BF16)
    tril = jnp.tril(jnp.ones((GLA_CHUNK, GLA_CHUNK), BF16))

    xf = x.reshape(n, d)
    for l in range(depth):
        proj, small = _inproj(xf, norm1_g[l][None], w_main[l], w_small[l], tm_big, 2048)
        y_gla, y_ml = _recurrent(proj, small, a_up[l], gla_a_b[l][None], gla_norm_g[l][None],
                                 mlstm_conv_w[l], ml_gate_b[l][None], mlstm_norm_g[l][None],
                                 tril, bsz, t, tb)
        qn, kn, vt, km = _moba_prep(proj, cos_t, sin_t, qn_g[l][None], kn_g[l][None], gmean, t)
        y_moba = _moba_attn(qn, kn, vt, km, bsz, t)
        xf = _merge(xf, y_gla, y_moba, y_ml, proj, gate_b[l][None], w_g[l], w_m[l], w_l[l],
                    w_o[l], tm)
        xf = _ffn(xf, norm2_g[l][None], w1[l], w2[l], tm_big, 2048)
    return xf.reshape(bsz, t, d)
```
